```python
import math
import jax, jax.numpy as jnp
from jax import lax
import numpy as np

D_MODEL = 1024
BATCH = 8
SEQ = 4096
DEPTH = 4

CHUNK = 64
N_MIXERS = 2
N_GDN_LAYERS = (DEPTH + N_MIXERS - 1) // N_MIXERS
N_SB_LAYERS = DEPTH // N_MIXERS

GDN_HEAD_DIM = 128
GDN_QK_HEADS = D_MODEL // GDN_HEAD_DIM
GDN_V_HEADS = 2 * GDN_QK_HEADS
GDN_KEY_DIM = GDN_QK_HEADS * GDN_HEAD_DIM
GDN_VAL_DIM = GDN_V_HEADS * GDN_HEAD_DIM
GDN_CONV_DIM = 2 * GDN_KEY_DIM + GDN_VAL_DIM
GDN_IN_DIM = GDN_CONV_DIM + GDN_VAL_DIM + 2 * GDN_V_HEADS
CONV_WIDTH = 4

SB_HEAD_DIM = 64
SB_HEADS = D_MODEL // SB_HEAD_DIM
Q_BLOCK = 128

FFN_HIDDEN = -(-(8 * D_MODEL) // (3 * 256)) * 256
EPS = 1e-6

kernel_name = "hybrid_gdn_stickbreaking_trunk"


def rmsnorm(x, w):
    xf = x.astype(jnp.float32)
    y = xf * lax.rsqrt(jnp.mean(xf * xf, axis=-1, keepdims=True) + EPS)
    return (y * w.astype(jnp.float32)).astype(x.dtype)


def l2norm(x):
    xf = x.astype(jnp.float32)
    return xf * lax.rsqrt(jnp.sum(xf * xf, axis=-1, keepdims=True) + EPS)


def causal_depthwise_conv(x, w):
    T = x.shape[1]
    xp = jnp.pad(x, ((0, 0), (CONV_WIDTH - 1, 0), (0, 0)))
    out = xp[:, 0:T, :] * w[:, 0]
    for j in range(1, CONV_WIDTH):
        out = out + xp[:, j:j + T, :] * w[:, j]
    return out


def _to_chunks(x, n_chunks):
    B, T, H = x.shape[:3]
    y = x.reshape((B, n_chunks, CHUNK, H) + x.shape[3:])
    return jnp.moveaxis(y, 3, 2)


def gated_delta_rule_chunked(q, k, v, g, beta):
    B, T, H, dk = q.shape
    dv = v.shape[-1]
    n = T // CHUNK
    q = q * (dk ** -0.5)
    qc, kc, vc = _to_chunks(q, n), _to_chunks(k, n), _to_chunks(v, n)
    gc = jnp.cumsum(_to_chunks(g, n), axis=-1)
    bc = _to_chunks(beta, n)
    k_beta = kc * bc[..., None]
    v_beta = vc * bc[..., None]

    incl = jnp.tril(jnp.ones((CHUNK, CHUNK), dtype=bool))
    strict = jnp.tril(jnp.ones((CHUNK, CHUNK), dtype=bool), -1)
    decay = jnp.exp(jnp.where(incl, gc[..., :, None] - gc[..., None, :], -jnp.inf))

    L = jnp.where(strict, jnp.einsum('bnhid,bnhjd->bnhij', k_beta, kc) * decay, 0.0)
    eye = jnp.eye(CHUNK, dtype=jnp.float32)
    rhs = jnp.concatenate([v_beta, k_beta * jnp.exp(gc)[..., None]], axis=-1)
    sol = lax.linalg.triangular_solve(L + eye, rhs, left_side=True, lower=True,
                                      unit_diagonal=True)
    u = sol[..., :dv]
    w = sol[..., dv:]
    attn_intra = jnp.einsum('bnhid,bnhjd->bnhij', qc, kc) * decay

    def step(S, xs):
        q_i, k_i, u_i, w_i, g_i, a_i = xs
        v_new = u_i - jnp.einsum('bhcd,bhde->bhce', w_i, S)
        o = (jnp.einsum('bhcd,bhde->bhce', q_i * jnp.exp(g_i)[..., None], S)
             + jnp.einsum('bhij,bhje->bhie', a_i, v_new))
        g_last = g_i[..., -1]
        k_dec = k_i * jnp.exp(g_last[..., None] - g_i)[..., None]
        S = S * jnp.exp(g_last)[..., None, None] + jnp.einsum('bhcd,bhce->bhde', k_dec, v_new)
        return S, o

    xs = tuple(jnp.moveaxis(t, 1, 0) for t in (qc, kc, u, w, gc, attn_intra))
    S0 = jnp.zeros((B, H, dk, dv), jnp.float32)
    _, o = lax.scan(step, S0, xs)
    return o.transpose(1, 0, 3, 2, 4).reshape(B, T, H, dv)


def gdn_mixer(h, w_in, conv_w, a_log, dt_bias, norm_w, w_out):
    B, T, _ = h.shape
    proj = h @ w_in
    qkv = proj[..., :GDN_CONV_DIM]
    z = proj[..., GDN_CONV_DIM:GDN_CONV_DIM + GDN_VAL_DIM]
    b = proj[..., GDN_CONV_DIM + GDN_VAL_DIM:GDN_CONV_DIM + GDN_VAL_DIM + GDN_V_HEADS]
    a = proj[..., GDN_CONV_DIM + GDN_VAL_DIM + GDN_V_HEADS:]
    qkv = jax.nn.silu(causal_depthwise_conv(qkv, conv_w))
    q = qkv[..., :GDN_KEY_DIM].reshape(B, T, GDN_QK_HEADS, GDN_HEAD_DIM)
    k = qkv[..., GDN_KEY_DIM:2 * GDN_KEY_DIM].reshape(B, T, GDN_QK_HEADS, GDN_HEAD_DIM)
    v = qkv[..., 2 * GDN_KEY_DIM:].reshape(B, T, GDN_V_HEADS, GDN_HEAD_DIM).astype(jnp.float32)
    rep = GDN_V_HEADS // GDN_QK_HEADS
    q = jnp.repeat(l2norm(q), rep, axis=2)
    k = jnp.repeat(l2norm(k), rep, axis=2)
    beta = jax.nn.sigmoid(b.astype(jnp.float32))
    g = -jnp.exp(a_log.astype(jnp.float32)) * jax.nn.softplus(
        a.astype(jnp.float32) + dt_bias.astype(jnp.float32))
    o = gated_delta_rule_chunked(q, k, v, g, beta)
    o = o * lax.rsqrt(jnp.mean(o * o, axis=-1, keepdims=True) + EPS) * norm_w.astype(jnp.float32)
    o = o * jax.nn.silu(z.reshape(B, T, GDN_V_HEADS, GDN_HEAD_DIM).astype(jnp.float32))
    return o.reshape(B, T, GDN_VAL_DIM).astype(h.dtype) @ w_out


def stick_breaking_mixer(h, w_qkv, w_o):
    B, T, _ = h.shape
    qkv = (h @ w_qkv).reshape(B, T, 3, SB_HEADS, SB_HEAD_DIM)
    q, k, v = qkv[:, :, 0], qkv[:, :, 1], qkv[:, :, 2]
    scale = SB_HEAD_DIM ** -0.5
    outs = []
    for blk in range(T // Q_BLOCK):
        t0 = blk * Q_BLOCK
        t1 = t0 + Q_BLOCK
        qb, kb, vb = q[:, t0:t1], k[:, :t1], v[:, :t1]
        z = jnp.einsum('bthd,bshd->bhts', qb, kb).astype(jnp.float32) * scale
        t_idx = t0 + jnp.arange(Q_BLOCK)[:, None]
        s_idx = jnp.arange(t1)[None, :]
        causal = s_idx < t_idx
        log_keep = jnp.where(causal, jax.nn.log_sigmoid(-z), 0.0)
        log_rest = lax.cumsum(log_keep, axis=3, reverse=True) - log_keep
        att = jnp.where(causal, jnp.exp(jax.nn.log_sigmoid(z) + log_rest), 0.0)
        outs.append(jnp.einsum('bhts,bshd->bthd', att.astype(v.dtype), vb))
    o = jnp.concatenate(outs, axis=1).reshape(B, T, SB_HEADS * SB_HEAD_DIM)
    return o @ w_o


def swiglu(h, w_gate, w_up, w_down):
    return (jax.nn.silu(h @ w_gate) * (h @ w_up)) @ w_down


def setup_inputs(seed: int = 0) -> dict:
    key = jax.random.key(seed)
    ks = jax.random.split(key, 20)
    nrm = jax.random.normal
    f32 = jnp.float32
    x = nrm(ks[0], (BATCH, SEQ, D_MODEL), f32)
    gdn_w_in = nrm(ks[1], (N_GDN_LAYERS, D_MODEL, GDN_IN_DIM), f32) * D_MODEL ** -0.5
    gdn_conv_w = nrm(ks[2], (N_GDN_LAYERS, GDN_CONV_DIM, CONV_WIDTH), f32) * CONV_WIDTH ** -0.5
    gdn_a_log = jnp.log(jax.random.uniform(ks[3], (N_GDN_LAYERS, GDN_V_HEADS), f32, 1.0, 16.0))
    dt = jnp.exp(jax.random.uniform(ks[4], (N_GDN_LAYERS, GDN_V_HEADS), f32,
                                    math.log(1e-3), math.log(1e-1)))
    gdn_dt_bias = dt + jnp.log(-jnp.expm1(-dt))
    gdn_norm_w = 1.0 + 0.02 * nrm(ks[5], (N_GDN_LAYERS, GDN_HEAD_DIM), f32)
    gdn_w_out = nrm(ks[6], (N_GDN_LAYERS, GDN_VAL_DIM, D_MODEL), f32) * GDN_VAL_DIM ** -0.5
    sb_w_qkv = nrm(ks[7], (N_SB_LAYERS, D_MODEL, 3 * D_MODEL), f32) * D_MODEL ** -0.5
    sb_w_o = nrm(ks[8], (N_SB_LAYERS, D_MODEL, D_MODEL), f32) * D_MODEL ** -0.5
    mix_norm_w = 1.0 + 0.02 * nrm(ks[9], (DEPTH, D_MODEL), f32)
    ffn_norm_w = 1.0 + 0.02 * nrm(ks[10], (DEPTH, D_MODEL), f32)
    ffn_w_gate = nrm(ks[11], (DEPTH, D_MODEL, FFN_HIDDEN), f32) * D_MODEL ** -0.5
    ffn_w_up = nrm(ks[12], (DEPTH, D_MODEL, FFN_HIDDEN), f32) * D_MODEL ** -0.5
    ffn_w_down = nrm(ks[13], (DEPTH, FFN_HIDDEN, D_MODEL), f32) * FFN_HIDDEN ** -0.5
    final_norm_w = 1.0 + 0.02 * nrm(ks[14], (D_MODEL,), f32)
    return {"x": x, "gdn_w_in": gdn_w_in, "gdn_conv_w": gdn_conv_w, "gdn_a_log": gdn_a_log,
            "gdn_dt_bias": gdn_dt_bias, "gdn_norm_w": gdn_norm_w, "gdn_w_out": gdn_w_out,
            "sb_w_qkv": sb_w_qkv, "sb_w_o": sb_w_o, "mix_norm_w": mix_norm_w,
            "ffn_norm_w": ffn_norm_w, "ffn_w_gate": ffn_w_gate, "ffn_w_up": ffn_w_up,
            "ffn_w_down": ffn_w_down, "final_norm_w": final_norm_w}


def reference(x, gdn_w_in, gdn_conv_w, gdn_a_log, gdn_dt_bias, gdn_norm_w, gdn_w_out,
              sb_w_qkv, sb_w_o, mix_norm_w, ffn_norm_w, ffn_w_gate, ffn_w_up, ffn_w_down,
              final_norm_w):
    for i in range(DEPTH):
        j = i // N_MIXERS
        h = rmsnorm(x, mix_norm_w[i])
        if i % N_MIXERS == 0:
            x = x + gdn_mixer(h, gdn_w_in[j], gdn_conv_w[j], gdn_a_log[j], gdn_dt_bias[j],
                              gdn_norm_w[j], gdn_w_out[j])
        else:
            x = x + stick_breaking_mixer(h, sb_w_qkv[j], sb_w_o[j])
        h = rmsnorm(x, ffn_norm_w[i])
        x = x + swiglu(h, ffn_w_gate[i], ffn_w_up[i], ffn_w_down[i])
    return rmsnorm(x, final_norm_w)
```

```python
import functools

import jax
import jax.numpy as jnp
from jax import lax
from jax.experimental import pallas as pl
from jax.experimental.pallas import tpu as pltpu

F32 = jnp.float32
BF16 = jnp.bfloat16

EPS = 1e-6
CHUNK = 64
GDN_HEAD_DIM = 128
SB_HEAD_DIM = 64
CONV_WIDTH = 4
LANES = 128
VMEM_LIMIT_BYTES = 48 * 1024 * 1024
EXP_ZERO_BELOW = -104.0


def _cparams(sem):
    return pltpu.CompilerParams(dimension_semantics=sem, vmem_limit_bytes=VMEM_LIMIT_BYTES)


def _silu(x):
    return x * (1.0 / (1.0 + jnp.exp(-x)))


def _softplus(x):
    return jnp.maximum(x, 0.0) + jnp.log(1.0 + jnp.exp(-jnp.abs(x)))


def _dot(a, b):
    return jnp.dot(a, b, preferred_element_type=F32)


def _dot_nt(a, b):
    return lax.dot_general(a, b, (((1,), (1,)), ((), ())), preferred_element_type=F32)


def _dot_tn(a, b):
    return lax.dot_general(a, b, (((0,), (0,)), ((), ())), preferred_element_type=F32)


def _dot_f32_lhs(x, m_bf16, terms):
    acc = None
    r = x
    for _ in range(terms):
        p = r.astype(BF16)
        d = _dot(p, m_bf16)
        acc = d if acc is None else acc + d
        r = r - p.astype(F32)
    return acc


def _norm_matmul_kernel(x_ref, nw_ref, w_ref, o_ref, h_ref):
    @pl.when(pl.program_id(1) == 0)
    def _():
        x = x_ref[...]
        y = x * lax.rsqrt(jnp.mean(x * x, axis=-1, keepdims=True) + EPS)
        h_ref[...] = (y * nw_ref[...]).astype(BF16)

    o_ref[...] = _dot(h_ref[...], w_ref[...]).astype(o_ref.dtype)


def norm_matmul(x, norm_w, w, *, tm, tn, out_dtype, tile_major):
    n, d = x.shape
    n_out = w.shape[1]
    grid = (n // tm, n_out // tn)
    if tile_major:
        out_shape = jax.ShapeDtypeStruct((n_out // tn, n, tn), out_dtype)
        out_spec = pl.BlockSpec((None, tm, tn), lambda i, j: (j, i, 0))
    else:
        out_shape = jax.ShapeDtypeStruct((n, n_out), out_dtype)
        out_spec = pl.BlockSpec((tm, tn), lambda i, j: (i, j))
    return pl.pallas_call(
        _norm_matmul_kernel,
        out_shape=out_shape,
        grid=grid,
        in_specs=[
            pl.BlockSpec((tm, d), lambda i, j: (i, 0)),
            pl.BlockSpec((1, d), lambda i, j: (0, 0)),
            pl.BlockSpec((d, tn), lambda i, j: (0, j)),
        ],
        out_specs=out_spec,
        scratch_shapes=[pltpu.VMEM((tm, d), BF16)],
        compiler_params=_cparams(("parallel", "arbitrary")),
        name="norm_matmul",
    )(x, norm_w.reshape(1, d), w)


def _matmul_residual_kernel(a_ref, w_ref, x_ref, o_ref):
    o_ref[...] = x_ref[...] + _dot(a_ref[...], w_ref[...])


def matmul_residual(a, w, x, *, tm):
    n, k = a.shape
    d = w.shape[1]
    return pl.pallas_call(
        _matmul_residual_kernel,
        out_shape=jax.ShapeDtypeStruct((n, d), F32),
        grid=(n // tm,),
        in_specs=[
            pl.BlockSpec((tm, k), lambda i: (i, 0)),
            pl.BlockSpec((k, d), lambda i: (0, 0)),
            pl.BlockSpec((tm, d), lambda i: (i, 0)),
        ],
        out_specs=pl.BlockSpec((tm, d), lambda i: (i, 0)),
        compiler_params=_cparams(("parallel",)),
        name="matmul_residual",
    )(a, w, x)


def _ffn_kernel(x_ref, nw_ref, wg_ref, wu_ref, wd_ref, fw_ref, o_ref, h_ref, acc_ref, *, final_norm):
    j = pl.program_id(1)

    @pl.when(j == 0)
    def _():
        x = x_ref[...]
        y = x * lax.rsqrt(jnp.mean(x * x, axis=-1, keepdims=True) + EPS)
        h_ref[...] = (y * nw_ref[...]).astype(BF16)
        acc_ref[...] = jnp.zeros_like(acc_ref)

    h = h_ref[...]
    g = _dot(h, wg_ref[...])
    u = _dot(h, wu_ref[...])
    a = (_silu(g) * u).astype(BF16)
    acc_ref[...] += _dot(a, wd_ref[...])

    @pl.when(j == pl.num_programs(1) - 1)
    def _():
        y = x_ref[...] + acc_ref[...]
        if final_norm:
            y = y * lax.rsqrt(jnp.mean(y * y, axis=-1, keepdims=True) + EPS) * fw_ref[...]
        o_ref[...] = y


def ffn_residual(x, norm_w, wg, wu, wd, final_w, *, tm, tf, final_norm):
    n, d = x.shape
    f = wg.shape[1]
    return pl.pallas_call(
        functools.partial(_ffn_kernel, final_norm=final_norm),
        out_shape=jax.ShapeDtypeStruct((n, d), F32),
        grid=(n // tm, f // tf),
        in_specs=[
            pl.BlockSpec((tm, d), lambda i, j: (i, 0)),
            pl.BlockSpec((1, d), lambda i, j: (0, 0)),
            pl.BlockSpec((d, tf), lambda i, j: (0, j)),
            pl.BlockSpec((d, tf), lambda i, j: (0, j)),
            pl.BlockSpec((tf, d), lambda i, j: (j, 0)),
            pl.BlockSpec((1, d), lambda i, j: (0, 0)),
        ],
        out_specs=pl.BlockSpec((tm, d), lambda i, j: (i, 0)),
        scratch_shapes=[pltpu.VMEM((tm, d), BF16), pltpu.VMEM((tm, d), F32)],
        compiler_params=_cparams(("parallel", "arbitrary")),
        name="ffn_residual",
    )(x, norm_w.reshape(1, d), wg, wu, wd, final_w.reshape(1, d))


GDN_TB = 256
GDN_NC = GDN_TB // CHUNK
HIST = 8
BETA_LANE = 0
A_LANE = 16


def _gdn_kernel(alog_ref, dtb_ref, q_ref, k_ref, v_ref, z_ref, ba_ref, cwq_ref, cwk_ref, cwv_ref,
                nw_ref, o_ref, xbuf, s_ref):
    hq = pl.program_id(1)
    t = pl.program_id(2)
    tb = GDN_TB
    dh = GDN_HEAD_DIM

    @pl.when(t == 0)
    def _():
        xbuf[0:HIST, :] = jnp.zeros((HIST, 4 * dh), F32)
        s_ref[...] = jnp.zeros_like(s_ref)

    xbuf[HIST:HIST + tb, 0:dh] = q_ref[...].astype(F32)
    xbuf[HIST:HIST + tb, dh:2 * dh] = k_ref[...].astype(F32)
    xbuf[HIST:HIST + tb, 2 * dh:4 * dh] = v_ref[...].astype(F32)

    def conv(lo, hi, cw_ref):
        acc = None
        for j in range(CONV_WIDTH):
            start = HIST - (CONV_WIDTH - 1) + j
            term = xbuf[start:start + tb, lo:hi] * cw_ref[j:j + 1, :]
            acc = term if acc is None else acc + term
        return _silu(acc)

    qc = conv(0, dh, cwq_ref)
    kc = conv(dh, 2 * dh, cwk_ref)
    vc = conv(2 * dh, 4 * dh, cwv_ref)
    xbuf[0:HIST, :] = xbuf[tb:tb + HIST, :]

    q = qc * lax.rsqrt(jnp.sum(qc * qc, axis=-1, keepdims=True) + EPS) * (dh ** -0.5)
    k = kc * lax.rsqrt(jnp.sum(kc * kc, axis=-1, keepdims=True) + EPS)
    q16 = q.astype(BF16)
    k16 = k.astype(BF16)

    lane = lax.broadcasted_iota(jnp.int32, (1, LANES), 1)
    ba = pltpu.roll(ba_ref[...], (LANES - 2 * hq) % LANES, 1)
    beta = 1.0 / (1.0 + jnp.exp(-ba))
    alog_row = jnp.where(lane == A_LANE, alog_ref[2 * hq], alog_ref[2 * hq + 1])
    dtb_row = jnp.where(lane == A_LANE, dtb_ref[2 * hq], dtb_ref[2 * hq + 1])
    g = -jnp.exp(alog_row) * _softplus(ba + dtb_row)

    ri = lax.broadcasted_iota(jnp.int32, (tb, tb), 0)
    ci = lax.broadcasted_iota(jnp.int32, (tb, tb), 1)
    shift = CHUNK.bit_length() - 1
    same_chunk = jnp.right_shift(ri, shift) == jnp.right_shift(ci, shift)
    tri = jnp.where(same_chunk, jnp.where(ri >= ci, 1.0, 0.0), 0.0).astype(BF16)
    gc = _dot_f32_lhs_rhs(tri, g)
    gc_t = gc.T
    beta_t = beta.T

    ii = lax.broadcasted_iota(jnp.int32, (CHUNK, CHUNK), 0)
    jj = lax.broadcasted_iota(jnp.int32, (CHUNK, CHUNK), 1)
    eye = jnp.where(ii == jj, 1.0, 0.0)

    nw = nw_ref[...]
    for c in range(GDN_NC):
        r0 = c * CHUNK
        qs = q16[r0:r0 + CHUNK]
        ks = k16[r0:r0 + CHUNK]
        kk = _dot_nt(ks, ks)
        qk = _dot_nt(qs, ks)
        for j in range(2):
            gcol = gc[r0:r0 + CHUNK, A_LANE + j:A_LANE + j + 1]
            grow = gc_t[A_LANE + j:A_LANE + j + 1, r0:r0 + CHUNK]
            bcol = beta[r0:r0 + CHUNK, BETA_LANE + j:BETA_LANE + j + 1]
            brow = beta_t[BETA_LANE + j:BETA_LANE + j + 1, r0:r0 + CHUNK]
            glast = grow[:, CHUNK - 1:CHUNK]
            diff = gcol - grow
            decay = jnp.where(ii >= jj, jnp.exp(jnp.where(ii >= jj, diff, 0.0)), 0.0)
            xm = jnp.where(ii > jj, -(bcol * kk * decay), 0.0)
            tinv = eye + xm
            xp = xm
            for _ in range(5):
                xp16 = xp.astype(BF16)
                xp = _dot(xp16, xp16)
                tinv = tinv + _dot(tinv.astype(BF16), xp.astype(BF16))
            vs = vc[r0:r0 + CHUNK, j * dh:(j + 1) * dh]
            u = _dot((tinv * brow).astype(BF16), vs.astype(BF16))
            w = _dot((tinv * (brow * jnp.exp(grow))).astype(BF16), ks)
            attn = qk * decay

            s = s_ref[j]
            s16 = s.astype(BF16)
            v_new = u - _dot(w.astype(BF16), s16)
            o = jnp.exp(gcol) * _dot(qs, s16) + _dot(attn.astype(BF16), v_new.astype(BF16))
            k_dec = (jnp.exp(glast - gcol) * v_new).astype(BF16)
            s_ref[j] = s * jnp.exp(glast) + _dot_tn(ks, k_dec)

            zs = z_ref[r0:r0 + CHUNK, j * dh:(j + 1) * dh].astype(F32)
            on = o * lax.rsqrt(jnp.mean(o * o, axis=-1, keepdims=True) + EPS) * nw
            o_ref[r0:r0 + CHUNK, j * dh:(j + 1) * dh] = (on * _silu(zs)).astype(o_ref.dtype)


def _dot_f32_lhs_rhs(m_bf16, x):
    acc = None
    r = x
    for _ in range(3):
        p = r.astype(BF16)
        d = _dot(m_bf16, p)
        acc = d if acc is None else acc + d
        r = r - p.astype(F32)
    return acc


def gdn_core(proj, ba, conv_w_t, a_log, dt_bias, norm_w, *, batch, seq):
    n = proj.shape[0]
    dh = GDN_HEAD_DIM
    nq = 8
    tpb = seq // GDN_TB
    row = lambda b, h, t: b * tpb + t
    return pl.pallas_call(
        _gdn_kernel,
        out_shape=jax.ShapeDtypeStruct((n, 2 * nq * dh), BF16),
        grid=(batch, nq, tpb),
        in_specs=[
            pl.BlockSpec(memory_space=pltpu.SMEM),
            pl.BlockSpec(memory_space=pltpu.SMEM),
            pl.BlockSpec((GDN_TB, dh), lambda b, h, t: (row(b, h, t), h)),
            pl.BlockSpec((GDN_TB, dh), lambda b, h, t: (row(b, h, t), nq + h)),
            pl.BlockSpec((GDN_TB, 2 * dh), lambda b, h, t: (row(b, h, t), nq + h)),
            pl.BlockSpec((GDN_TB, 2 * dh), lambda b, h, t: (row(b, h, t), 2 * nq + h)),
            pl.BlockSpec((GDN_TB, LANES), lambda b, h, t: (row(b, h, t), 0)),
            pl.BlockSpec((CONV_WIDTH, dh), lambda b, h, t: (0, h)),
            pl.BlockSpec((CONV_WIDTH, dh), lambda b, h, t: (0, nq + h)),
            pl.BlockSpec((CONV_WIDTH, 2 * dh), lambda b, h, t: (0, nq + h)),
            pl.BlockSpec((1, dh), lambda b, h, t: (0, 0)),
        ],
        out_specs=pl.BlockSpec((GDN_TB, 2 * dh), lambda b, h, t: (row(b, h, t), h)),
        scratch_shapes=[
            pltpu.VMEM((GDN_TB + HIST, 4 * dh), F32),
            pltpu.VMEM((2, dh, dh), F32),
        ],
        compiler_params=_cparams(("parallel", "parallel", "arbitrary")),
        name="gdn_core",
    )(a_log, dt_bias, proj, proj, proj, proj, ba, conv_w_t, conv_w_t, conv_w_t,
      norm_w.reshape(1, dh))


SB_TQ = 256
SB_TS = 128


def _sb_kernel(q_ref, k_ref, v_ref, o_ref, *, early_exit):
    i = pl.program_id(2)
    tq, ts, dh = SB_TQ, SB_TS, SB_HEAD_DIM
    scale = dh ** -0.5
    lane = lax.broadcasted_iota(jnp.int32, (1, 2 * dh), 1)
    q = q_ref[...]
    zero16 = jnp.zeros((), BF16)
    qh = [jnp.where(lane < dh, q, zero16), jnp.where(lane >= dh, q, zero16)]
    head_mask = [lane < dh, lane >= dh]

    ri = lax.broadcasted_iota(jnp.int32, (ts, ts), 0)
    ci = lax.broadcasted_iota(jnp.int32, (ts, ts), 1)
    upper = jnp.where(ri > ci, 1.0, 0.0).astype(BF16)
    t_idx = i * tq + lax.broadcasted_iota(jnp.int32, (tq, ts), 0)
    s_loc = lax.broadcasted_iota(jnp.int32, (tq, ts), 1)

    n_blocks = (i + 1) * (tq // ts)

    def body(carry):
        step, acc, rest0, rest1 = carry
        jb = n_blocks - 1 - step
        s0 = pl.multiple_of(jb * ts, ts)
        kb = k_ref[pl.ds(s0, ts), :]
        vb = v_ref[pl.ds(s0, ts), :]
        causal = (s0 + s_loc) < t_idx
        rests = [rest0, rest1]
        new_rests = []
        for h in range(2):
            z = _dot_nt(qh[h], kb) * scale
            lk = jnp.where(causal, -_softplus(z), 0.0)
            rest = _dot_f32_lhs(lk, upper, 2) + rests[h]
            att = jnp.where(causal, jnp.exp(z + lk + rest), 0.0)
            vh = jnp.where(head_mask[h], vb, zero16)
            acc = acc + _dot(att.astype(BF16), vh)
            new_rests.append(rests[h] + jnp.sum(lk, axis=-1, keepdims=True))
        return step + 1, acc, new_rests[0], new_rests[1]

    def cond(carry):
        step, _, rest0, rest1 = carry
        more = step < n_blocks
        if early_exit:
            alive = jnp.maximum(jnp.max(rest0), jnp.max(rest1)) >= EXP_ZERO_BELOW
            more = jnp.logical_and(more, alive)
        return more

    init = (jnp.int32(0), jnp.zeros((tq, 2 * dh), F32), jnp.zeros((tq, 1), F32), jnp.zeros((tq, 1), F32))
    _, acc, _, _ = lax.while_loop(cond, body, init)
    o_ref[...] = acc.astype(o_ref.dtype)


def sb_attention(qkv, *, batch, seq, early_exit):
    n = qkv.shape[1]
    npair = 8
    qpb = seq // SB_TQ
    return pl.pallas_call(
        functools.partial(_sb_kernel, early_exit=early_exit),
        out_shape=jax.ShapeDtypeStruct((n, npair * LANES), BF16),
        grid=(batch, npair, qpb),
        in_specs=[
            pl.BlockSpec((None, SB_TQ, LANES), lambda b, p, i: (p, b * qpb + i, 0)),
            pl.BlockSpec((None, seq, LANES), lambda b, p, i: (npair + p, b, 0)),
            pl.BlockSpec((None, seq, LANES), lambda b, p, i: (2 * npair + p, b, 0)),
        ],
        out_specs=pl.BlockSpec((SB_TQ, LANES), lambda b, p, i: (b * qpb + i, p)),
        compiler_params=_cparams(("parallel", "parallel", "arbitrary")),
        name="sb_attention",
    )(qkv, qkv, qkv)


def kernel(x, gdn_w_in, gdn_conv_w, gdn_a_log, gdn_dt_bias, gdn_norm_w, gdn_w_out, sb_w_qkv, sb_w_o,
           mix_norm_w, ffn_norm_w, ffn_w_gate, ffn_w_up, ffn_w_down, final_norm_w):
    batch, seq, d = x.shape
    n = batch * seq
    depth = mix_norm_w.shape[0]
    n_proj = 6 * d
    xf = x.reshape(n, d)
    for i in range(depth):
        j = i // 2
        if i % 2 == 0:
            w_in = gdn_w_in[j]
            w_main = w_in[:, :n_proj].astype(BF16)
            w_ba = jnp.pad(w_in[:, n_proj:], ((0, 0), (0, LANES - (w_in.shape[1] - n_proj)))).astype(BF16)
            proj = norm_matmul(xf, mix_norm_w[i], w_main, tm=512, tn=512, out_dtype=F32, tile_major=False)
            ba = norm_matmul(xf, mix_norm_w[i], w_ba, tm=512, tn=LANES, out_dtype=F32, tile_major=False)
            og = gdn_core(proj, ba, gdn_conv_w[j].T, gdn_a_log[j], gdn_dt_bias[j], gdn_norm_w[j],
                          batch=batch, seq=seq)
            xf = matmul_residual(og, gdn_w_out[j].astype(BF16), xf, tm=512)
        else:
            qkv = norm_matmul(xf, mix_norm_w[i], sb_w_qkv[j].astype(BF16), tm=512, tn=LANES,
                              out_dtype=BF16, tile_major=True)
            o = sb_attention(qkv, batch=batch, seq=seq, early_exit=False)
            xf = matmul_residual(o, sb_w_o[j].astype(BF16), xf, tm=512)
        xf = ffn_residual(xf, ffn_norm_w[i], ffn_w_gate[i].astype(BF16), ffn_w_up[i].astype(BF16),
                          ffn_w_down[i].astype(BF16), final_norm_w, tm=512, tf=256,
                          final_norm=(i == depth - 1))
    return xf.reshape(batch, seq, d)
```

```python
import functools
import math

import jax
import jax.numpy as jnp
from jax import lax
from jax.experimental import pallas as pl
from jax.experimental.pallas import tpu as pltpu

F32 = jnp.float32
BF16 = jnp.bfloat16

EPS = 1e-6
GDN_HEAD_DIM = 128
SB_HEAD_DIM = 64
CONV_WIDTH = 4
LANES = 128
VMEM_LIMIT_BYTES = 48 * 1024 * 1024
LOG2E = 1.4426950408889634
EXP2_ZERO_BELOW = -104.0 * LOG2E


def _cparams(sem):
    return pltpu.CompilerParams(dimension_semantics=sem, vmem_limit_bytes=VMEM_LIMIT_BYTES)


def _silu(x):
    return x * (1.0 / (1.0 + jnp.exp(-x)))


def _softplus(x):
    return jnp.maximum(x, 0.0) + jnp.log(1.0 + jnp.exp(-jnp.abs(x)))


def _dot(a, b):
    return jnp.dot(a, b, preferred_element_type=F32)


def _dot_nt(a, b):
    return lax.dot_general(a, b, (((1,), (1,)), ((), ())), preferred_element_type=F32)


def _dot_tn(a, b):
    return lax.dot_general(a, b, (((0,), (0,)), ((), ())), preferred_element_type=F32)


def _split_bf16(x, terms):
    parts = []
    r = x
    for _ in range(terms):
        p = r.astype(BF16)
        parts.append(p)
        r = r - p.astype(F32)
    return parts


def _norm_matmul_kernel(x_ref, nw_ref, w_ref, o_ref, h_ref):
    @pl.when(pl.program_id(1) == 0)
    def _():
        x = x_ref[...]
        y = x * lax.rsqrt(jnp.mean(x * x, axis=-1, keepdims=True) + EPS)
        h_ref[...] = (y * nw_ref[...]).astype(BF16)

    o_ref[...] = _dot(h_ref[...], w_ref[...]).astype(o_ref.dtype)


def norm_matmul(x, norm_w, w, *, tm, tn, out_dtype, tile_major):
    n, d = x.shape
    n_out = w.shape[1]
    grid = (n // tm, n_out // tn)
    if tile_major:
        out_shape = jax.ShapeDtypeStruct((n_out // tn, n, tn), out_dtype)
        out_spec = pl.BlockSpec((None, tm, tn), lambda i, j: (j, i, 0))
    else:
        out_shape = jax.ShapeDtypeStruct((n, n_out), out_dtype)
        out_spec = pl.BlockSpec((tm, tn), lambda i, j: (i, j))
    return pl.pallas_call(
        _norm_matmul_kernel,
        out_shape=out_shape,
        grid=grid,
        in_specs=[
            pl.BlockSpec((tm, d), lambda i, j: (i, 0)),
            pl.BlockSpec((1, d), lambda i, j: (0, 0)),
            pl.BlockSpec((d, tn), lambda i, j: (0, j)),
        ],
        out_specs=out_spec,
        scratch_shapes=[pltpu.VMEM((tm, d), BF16)],
        compiler_params=_cparams(("parallel", "arbitrary")),
        name="norm_matmul",
    )(x, norm_w.reshape(1, d), w)


def _matmul_residual_kernel(a_ref, w_ref, x_ref, o_ref):
    o_ref[...] = x_ref[...] + _dot(a_ref[...], w_ref[...])


def matmul_residual(a, w, x, *, tm):
    n, k = a.shape
    d = w.shape[1]
    return pl.pallas_call(
        _matmul_residual_kernel,
        out_shape=jax.ShapeDtypeStruct((n, d), F32),
        grid=(n // tm,),
        in_specs=[
            pl.BlockSpec((tm, k), lambda i: (i, 0)),
            pl.BlockSpec((k, d), lambda i: (0, 0)),
            pl.BlockSpec((tm, d), lambda i: (i, 0)),
        ],
        out_specs=pl.BlockSpec((tm, d), lambda i: (i, 0)),
        compiler_params=_cparams(("parallel",)),
        name="matmul_residual",
    )(a, w, x)


def _ffn_kernel(x_ref, nw_ref, wg_ref, wu_ref, wd_ref, fw_ref, o_ref, h_ref, acc_ref, *, final_norm):
    j = pl.program_id(1)

    @pl.when(j == 0)
    def _():
        x = x_ref[...]
        y = x * lax.rsqrt(jnp.mean(x * x, axis=-1, keepdims=True) + EPS)
        h_ref[...] = (y * nw_ref[...]).astype(BF16)
        acc_ref[...] = jnp.zeros_like(acc_ref)

    h = h_ref[...]
    g = _dot(h, wg_ref[...])
    u = _dot(h, wu_ref[...])
    a = (_silu(g) * u).astype(BF16)
    acc_ref[...] += _dot(a, wd_ref[...])

    @pl.when(j == pl.num_programs(1) - 1)
    def _():
        y = x_ref[...] + acc_ref[...]
        if final_norm:
            y = y * lax.rsqrt(jnp.mean(y * y, axis=-1, keepdims=True) + EPS) * fw_ref[...]
        o_ref[...] = y


def ffn_residual(x, norm_w, wg, wu, wd, final_w, *, tm, tf, final_norm):
    n, d = x.shape
    f = wg.shape[1]
    return pl.pallas_call(
        functools.partial(_ffn_kernel, final_norm=final_norm),
        out_shape=jax.ShapeDtypeStruct((n, d), F32),
        grid=(n // tm, f // tf),
        in_specs=[
            pl.BlockSpec((tm, d), lambda i, j: (i, 0)),
            pl.BlockSpec((1, d), lambda i, j: (0, 0)),
            pl.BlockSpec((d, tf), lambda i, j: (0, j)),
            pl.BlockSpec((d, tf), lambda i, j: (0, j)),
            pl.BlockSpec((tf, d), lambda i, j: (j, 0)),
            pl.BlockSpec((1, d), lambda i, j: (0, 0)),
        ],
        out_specs=pl.BlockSpec((tm, d), lambda i, j: (i, 0)),
        scratch_shapes=[pltpu.VMEM((tm, d), BF16), pltpu.VMEM((tm, d), F32)],
        compiler_params=_cparams(("parallel", "arbitrary")),
        name="ffn_residual",
    )(x, norm_w.reshape(1, d), wg, wu, wd, final_w.reshape(1, d))


GDN_TB = 256
GDN_LEVELS = GDN_TB.bit_length() - 1
HIST = 8
BETA_LANE = 0
A_LANE = 16


def _gdn_kernel(alog_ref, dtb_ref, q_ref, k_ref, v_ref, z_ref, ba_ref, cwq_ref, cwk_ref, cwv_ref,
                nw_ref, o_ref, xbuf, s_ref):
    hq = pl.program_id(1)
    t = pl.program_id(2)
    tb = GDN_TB
    dh = GDN_HEAD_DIM

    @pl.when(t == 0)
    def _():
        xbuf[0:HIST, :] = jnp.zeros((HIST, 4 * dh), F32)
        s_ref[...] = jnp.zeros_like(s_ref)

    xbuf[HIST:HIST + tb, 0:dh] = q_ref[...].astype(F32)
    xbuf[HIST:HIST + tb, dh:2 * dh] = k_ref[...].astype(F32)
    xbuf[HIST:HIST + tb, 2 * dh:4 * dh] = v_ref[...].astype(F32)

    def conv(lo, hi, cw_ref):
        acc = None
        for j in range(CONV_WIDTH):
            start = HIST - (CONV_WIDTH - 1) + j
            term = xbuf[start:start + tb, lo:hi] * cw_ref[j:j + 1, :]
            acc = term if acc is None else acc + term
        return _silu(acc)

    qc = conv(0, dh, cwq_ref)
    kc = conv(dh, 2 * dh, cwk_ref)
    vc = conv(2 * dh, 4 * dh, cwv_ref)
    xbuf[0:HIST, :] = xbuf[tb:tb + HIST, :]

    q = qc * lax.rsqrt(jnp.sum(qc * qc, axis=-1, keepdims=True) + EPS) * (dh ** -0.5)
    k = kc * lax.rsqrt(jnp.sum(kc * kc, axis=-1, keepdims=True) + EPS)
    q16 = q.astype(BF16)
    k16 = k.astype(BF16)

    lane = lax.broadcasted_iota(jnp.int32, (1, LANES), 1)
    ba = pltpu.roll(ba_ref[...], (LANES - 2 * hq) % LANES, 1)
    beta = 1.0 / (1.0 + jnp.exp(-ba))
    alog_row = jnp.where(lane == A_LANE, alog_ref[2 * hq], alog_ref[2 * hq + 1])
    dtb_row = jnp.where(lane == A_LANE, dtb_ref[2 * hq], dtb_ref[2 * hq + 1])
    g = -jnp.exp(alog_row) * _softplus(ba + dtb_row)

    ii = lax.broadcasted_iota(jnp.int32, (tb, tb), 0)
    jj = lax.broadcasted_iota(jnp.int32, (tb, tb), 1)
    lower = ii >= jj
    strict = ii > jj
    ij_xor = jnp.bitwise_xor(ii, jj)
    tri =jnp.where(lower, 1.0, 0.0).astype(BF16)
    gc = sum(_dot(tri, p) for p in _split_bf16(g, 3))
    gc_t = gc.T

    kk = _dot_nt(k16, k16)
    qk = _dot_nt(q16, k16)
    nw = nw_ref[...]
    for j in range(2):
        gcol = gc[:, A_LANE + j:A_LANE + j + 1]
        grow = gc_t[A_LANE + j:A_LANE + j + 1, :]
        bcol = beta[:, BETA_LANE + j:BETA_LANE + j + 1]
        glast = grow[:, tb - 1:tb]
        decay = jnp.where(lower, jnp.exp(gcol - grow), 0.0)
        lmat = jnp.where(strict, (bcol * kk) * decay, 0.0)
        dinv = jnp.where(ii == jj, 1.0, jnp.where(ij_xor == 1, -lmat, 0.0))
        for level in range(1, GDN_LEVELS):
            c16 = jnp.where(jnp.right_shift(ij_xor, level) == 1, lmat, 0.0).astype(BF16)
            d16 = dinv.astype(BF16)
            dinv = dinv - _dot(d16, _dot(c16, d16).astype(BF16))
        r16 = jnp.where(ii == jj, 0.0, dinv).astype(BF16)
        vb = bcol * vc[:, j * dh:(j + 1) * dh]
        kb = (bcol * jnp.exp(gcol)) * k
        u = vb + _dot(r16, vb.astype(BF16))
        w = kb + _dot(r16, kb.astype(BF16))
        attn = (qk * decay).astype(BF16)

        s = s_ref[j]
        s16 = s.astype(BF16)
        v_new = u - _dot(w.astype(BF16), s16)
        o = jnp.exp(gcol) * _dot(q16, s16) + _dot(attn, v_new.astype(BF16))
        k_dec = (jnp.exp(glast - gcol) * v_new).astype(BF16)
        s_ref[j] = s * jnp.exp(glast) + _dot_tn(k16, k_dec)

        zs = z_ref[:, j * dh:(j + 1) * dh].astype(F32)
        on = o * lax.rsqrt(jnp.mean(o * o, axis=-1, keepdims=True) + EPS) * nw
        o_ref[:, j * dh:(j + 1) * dh] = (on * _silu(zs)).astype(o_ref.dtype)


def gdn_core(proj, ba, conv_w_t, a_log, dt_bias, norm_w, *, batch, seq):
    n = proj.shape[0]
    dh = GDN_HEAD_DIM
    nq = 8
    tpb = seq // GDN_TB
    row = lambda b, h, t: b * tpb + t
    return pl.pallas_call(
        _gdn_kernel,
        out_shape=jax.ShapeDtypeStruct((n, 2 * nq * dh), BF16),
        grid=(batch, nq, tpb),
        in_specs=[
            pl.BlockSpec(memory_space=pltpu.SMEM),
            pl.BlockSpec(memory_space=pltpu.SMEM),
            pl.BlockSpec((GDN_TB, dh), lambda b, h, t: (row(b, h, t), h)),
            pl.BlockSpec((GDN_TB, dh), lambda b, h, t: (row(b, h, t), nq + h)),
            pl.BlockSpec((GDN_TB, 2 * dh), lambda b, h, t: (row(b, h, t), nq + h)),
            pl.BlockSpec((GDN_TB, 2 * dh), lambda b, h, t: (row(b, h, t), 2 * nq + h)),
            pl.BlockSpec((GDN_TB, LANES), lambda b, h, t: (row(b, h, t), 0)),
            pl.BlockSpec((CONV_WIDTH, dh), lambda b, h, t: (0, h)),
            pl.BlockSpec((CONV_WIDTH, dh), lambda b, h, t: (0, nq + h)),
            pl.BlockSpec((CONV_WIDTH, 2 * dh), lambda b, h, t: (0, nq + h)),
            pl.BlockSpec((1, dh), lambda b, h, t: (0, 0)),
        ],
        out_specs=pl.BlockSpec((GDN_TB, 2 * dh), lambda b, h, t: (row(b, h, t), h)),
        scratch_shapes=[
            pltpu.VMEM((GDN_TB + HIST, 4 * dh), F32),
            pltpu.VMEM((2, dh, dh), F32),
        ],
        compiler_params=_cparams(("parallel", "parallel", "arbitrary")),
        name="gdn_core",
    )(a_log, dt_bias, proj, proj, proj, proj, ba, conv_w_t, conv_w_t, conv_w_t,
      norm_w.reshape(1, dh))


SB_T = 256


def _sb_kernel(q_ref, k_ref, v_ref, o_ref):
    i = pl.program_id(2)
    tb, dh = SB_T, SB_HEAD_DIM
    lane = lax.broadcasted_iota(jnp.int32, (1, 2 * dh), 1)
    head_mask = [lane < dh, lane >= dh]
    zero16 = jnp.zeros((), BF16)
    q = q_ref[...] * jnp.asarray(dh ** -0.5, BF16)
    qh = [jnp.where(m, q, zero16) for m in head_mask]

    ri = lax.broadcasted_iota(jnp.int32, (tb, tb), 0)
    ci = lax.broadcasted_iota(jnp.int32, (tb, tb), 1)
    upper = jnp.where(ri > ci, 1.0, 0.0).astype(BF16)
    causal = ci < ri

    def block(jb, acc, sums, masked):
        s0 = pl.multiple_of(jb * tb, tb)
        kb = k_ref[pl.ds(s0, tb), :]
        vb = v_ref[pl.ds(s0, tb), :]
        new_sums = []
        for h in range(2):
            z2 = _dot_nt(qh[h], kb) * LOG2E
            sp2 = jnp.maximum(z2, 0.0) + jnp.log2(1.0 + jnp.exp2(-jnp.abs(z2)))
            if masked:
                sp2 = jnp.where(causal, sp2, 0.0)
            within = sum(_dot(p, upper) for p in _split_bf16(sp2, 2))
            att = jnp.exp2(z2 - sp2 - within - sums[h])
            if masked:
                att = jnp.where(causal, att, 0.0)
            acc = acc + _dot(att.astype(BF16), jnp.where(head_mask[h], vb, zero16))
            new_sums.append(sums[h] + jnp.sum(sp2, axis=-1, keepdims=True))
        return acc, new_sums

    zeros = jnp.zeros((tb, 1), F32)
    acc, sums = block(i, jnp.zeros((tb, 2 * dh), F32), [zeros, zeros], masked=True)

    def cond(carry):
        jb, _, sum0, sum1 = carry
        alive = jnp.minimum(jnp.min(sum0), jnp.min(sum1)) <= -EXP2_ZERO_BELOW
        return jnp.logical_and(jb >= 0, alive)

    def body(carry):
        jb, acc, sum0, sum1 = carry
        acc, (sum0, sum1) = block(jb, acc, [sum0, sum1], masked=False)
        return jb - 1, acc, sum0, sum1

    _, acc, _, _ = lax.while_loop(cond, body, (i - 1, acc, sums[0], sums[1]))
    o_ref[...] = acc.astype(o_ref.dtype)


def sb_attention(qkv, *, batch, seq):
    n = qkv.shape[1]
    npair = 8
    qpb = seq // SB_T
    return pl.pallas_call(
        _sb_kernel,
        out_shape=jax.ShapeDtypeStruct((n, npair * LANES), BF16),
        grid=(batch, npair, qpb),
        in_specs=[
            pl.BlockSpec((None, SB_T, LANES), lambda b, p, i: (p, b * qpb + i, 0)),
            pl.BlockSpec((None, seq, LANES), lambda b, p, i: (npair + p, b, 0)),
            pl.BlockSpec((None, seq, LANES), lambda b, p, i: (2 * npair + p, b, 0)),
        ],
        out_specs=pl.BlockSpec((SB_T, LANES), lambda b, p, i: (b * qpb + i, p)),
        compiler_params=_cparams(("parallel", "parallel", "arbitrary")),
        name="sb_attention",
    )(qkv, qkv, qkv)


def kernel(x, gdn_w_in, gdn_conv_w, gdn_a_log, gdn_dt_bias, gdn_norm_w, gdn_w_out, sb_w_qkv, sb_w_o,
           mix_norm_w, ffn_norm_w, ffn_w_gate, ffn_w_up, ffn_w_down, final_norm_w):
    batch, seq, d = x.shape
    n = batch * seq
    depth = mix_norm_w.shape[0]
    n_proj = 6 * d
    xf = x.reshape(n, d)
    for i in range(depth):
        j = i // 2
        if i % 2 == 0:
            w_in = gdn_w_in[j]
            w_main = w_in[:, :n_proj].astype(BF16)
            w_ba = jnp.pad(w_in[:, n_proj:], ((0, 0), (0, LANES - (w_in.shape[1] - n_proj)))).astype(BF16)
            proj = norm_matmul(xf, mix_norm_w[i], w_main, tm=512, tn=512, out_dtype=F32, tile_major=False)
            ba = norm_matmul(xf, mix_norm_w[i], w_ba, tm=512, tn=LANES, out_dtype=F32, tile_major=False)
            og = gdn_core(proj, ba, gdn_conv_w[j].T, gdn_a_log[j], gdn_dt_bias[j], gdn_norm_w[j],
                          batch=batch, seq=seq)
            xf = matmul_residual(og, gdn_w_out[j].astype(BF16), xf, tm=512)
        else:
            qkv = norm_matmul(xf, mix_norm_w[i], sb_w_qkv[j].astype(BF16), tm=512, tn=LANES,
                              out_dtype=BF16, tile_major=True)
            o = sb_attention(qkv, batch=batch, seq=seq)
            xf = matmul_residual(o, sb_w_o[j].astype(BF16), xf, tm=512)
        xf = ffn_residual(xf, ffn_norm_w[i], ffn_w_gate[i].astype(BF16), ffn_w_up[i].astype(BF16),
                          ffn_w_down[i].astype(BF16), final_norm_w, tm=512, tf=256,
                          final_norm=(i == depth - 1))
    return xf.reshape(batch, seq, d)
```

```python
import functools
import math

import jax
import jax.numpy as jnp
from jax import lax
from jax.experimental import pallas as pl
from jax.experimental.pallas import tpu as pltpu

F32 = jnp.float32
BF16 = jnp.bfloat16

EPS = 1e-6
GDN_HEAD_DIM = 128
SB_HEAD_DIM = 64
CONV_WIDTH = 4
LANES = 128
VMEM_LIMIT_BYTES = 48 * 1024 * 1024
LOG2E = 1.4426950408889634
EXP2_ZERO_BELOW = -104.0 * LOG2E


def _cparams(sem):
    return pltpu.CompilerParams(dimension_semantics=sem, vmem_limit_bytes=VMEM_LIMIT_BYTES)


def _silu(x):
    return x * (1.0 / (1.0 + jnp.exp(-x)))


def _softplus(x):
    return jnp.maximum(x, 0.0) + jnp.log(1.0 + jnp.exp(-jnp.abs(x)))


def _dot(a, b):
    return jnp.dot(a, b, preferred_element_type=F32)


def _dot_nt(a, b):
    return lax.dot_general(a, b, (((1,), (1,)), ((), ())), preferred_element_type=F32)


def _dot_tn(a, b):
    return lax.dot_general(a, b, (((0,), (0,)), ((), ())), preferred_element_type=F32)


def _split_bf16(x, terms):
    parts = []
    r = x
    for _ in range(terms):
        p = r.astype(BF16)
        parts.append(p)
        r = r - p.astype(F32)
    return parts


def _norm_matmul_kernel(x_ref, nw_ref, w_ref, o_ref, h_ref):
    @pl.when(pl.program_id(1) == 0)
    def _():
        x = x_ref[...]
        y = x * lax.rsqrt(jnp.mean(x * x, axis=-1, keepdims=True) + EPS)
        h_ref[...] = (y * nw_ref[...]).astype(BF16)

    o_ref[...] = _dot(h_ref[...], w_ref[...]).astype(o_ref.dtype)


def norm_matmul(x, norm_w, w, *, tm, tn, out_dtype, tile_major):
    n, d = x.shape
    n_out = w.shape[1]
    grid = (n // tm, n_out // tn)
    if tile_major:
        out_shape = jax.ShapeDtypeStruct((n_out // tn, n, tn), out_dtype)
        out_spec = pl.BlockSpec((None, tm, tn), lambda i, j: (j, i, 0))
    else:
        out_shape = jax.ShapeDtypeStruct((n, n_out), out_dtype)
        out_spec = pl.BlockSpec((tm, tn), lambda i, j: (i, j))
    return pl.pallas_call(
        _norm_matmul_kernel,
        out_shape=out_shape,
        grid=grid,
        in_specs=[
            pl.BlockSpec((tm, d), lambda i, j: (i, 0)),
            pl.BlockSpec((1, d), lambda i, j: (0, 0)),
            pl.BlockSpec((d, tn), lambda i, j: (0, j)),
        ],
        out_specs=out_spec,
        scratch_shapes=[pltpu.VMEM((tm, d), BF16)],
        compiler_params=_cparams(("parallel", "arbitrary")),
        name="norm_matmul",
    )(x, norm_w.reshape(1, d), w)


def _matmul_residual_kernel(a_ref, w_ref, x_ref, o_ref):
    o_ref[...] = x_ref[...] + _dot(a_ref[...], w_ref[...])


def matmul_residual(a, w, x, *, tm):
    n, k = a.shape
    d = w.shape[1]
    return pl.pallas_call(
        _matmul_residual_kernel,
        out_shape=jax.ShapeDtypeStruct((n, d), F32),
        grid=(n // tm,),
        in_specs=[
            pl.BlockSpec((tm, k), lambda i: (i, 0)),
            pl.BlockSpec((k, d), lambda i: (0, 0)),
            pl.BlockSpec((tm, d), lambda i: (i, 0)),
        ],
        out_specs=pl.BlockSpec((tm, d), lambda i: (i, 0)),
        compiler_params=_cparams(("parallel",)),
        name="matmul_residual",
    )(a, w, x)


def _ffn_kernel(x_ref, nw_ref, wg_ref, wu_ref, wd_ref, fw_ref, o_ref, h_ref, acc_ref, *, final_norm):
    j = pl.program_id(1)

    @pl.when(j == 0)
    def _():
        x = x_ref[...]
        y = x * lax.rsqrt(jnp.mean(x * x, axis=-1, keepdims=True) + EPS)
        h_ref[...] = (y * nw_ref[...]).astype(BF16)
        acc_ref[...] = jnp.zeros_like(acc_ref)

    h = h_ref[...]
    g = _dot(h, wg_ref[...])
    u = _dot(h, wu_ref[...])
    a = (_silu(g) * u).astype(BF16)
    acc_ref[...] += _dot(a, wd_ref[...])

    @pl.when(j == pl.num_programs(1) - 1)
    def _():
        y = x_ref[...] + acc_ref[...]
        if final_norm:
            y = y * lax.rsqrt(jnp.mean(y * y, axis=-1, keepdims=True) + EPS) * fw_ref[...]
        o_ref[...] = y


def ffn_residual(x, norm_w, wg, wu, wd, final_w, *, tm, tf, final_norm):
    n, d = x.shape
    f = wg.shape[1]
    return pl.pallas_call(
        functools.partial(_ffn_kernel, final_norm=final_norm),
        out_shape=jax.ShapeDtypeStruct((n, d), F32),
        grid=(n // tm, f // tf),
        in_specs=[
            pl.BlockSpec((tm, d), lambda i, j: (i, 0)),
            pl.BlockSpec((1, d), lambda i, j: (0, 0)),
            pl.BlockSpec((d, tf), lambda i, j: (0, j)),
            pl.BlockSpec((d, tf), lambda i, j: (0, j)),
            pl.BlockSpec((tf, d), lambda i, j: (j, 0)),
            pl.BlockSpec((1, d), lambda i, j: (0, 0)),
        ],
        out_specs=pl.BlockSpec((tm, d), lambda i, j: (i, 0)),
        scratch_shapes=[pltpu.VMEM((tm, d), BF16), pltpu.VMEM((tm, d), F32)],
        compiler_params=_cparams(("parallel", "arbitrary")),
        name="ffn_residual",
    )(x, norm_w.reshape(1, d), wg, wu, wd, final_w.reshape(1, d))


GDN_TB = 256
GDN_LEVELS = GDN_TB.bit_length() - 1
GDN_HB = 2
GDN_QK_HEADS = 8
HIST = 8
BETA_LANE = 0
A_LANE = 16


def _gdn_kernel(alog_ref, dtb_ref, q_ref, k_ref, v_ref, z_ref, ba_ref, cwq_ref, cwk_ref, cwv_ref,
                nw_ref, o_ref, xbuf, s_ref):
    hq = pl.program_id(1)
    t = pl.program_id(2)
    tb = GDN_TB
    dh = GDN_HEAD_DIM
    hb = GDN_HB
    k_off = hb * dh
    v_off = 2 * hb * dh

    @pl.when(t == 0)
    def _():
        xbuf[0:HIST, :] = jnp.zeros((HIST, 4 * hb * dh), F32)
        s_ref[...] = jnp.zeros_like(s_ref)

    xbuf[HIST:HIST + tb, 0:k_off] = q_ref[...].astype(F32)
    xbuf[HIST:HIST + tb, k_off:v_off] = k_ref[...].astype(F32)
    xbuf[HIST:HIST + tb, v_off:] = v_ref[...].astype(F32)

    def conv(lo, width, cw_ref, cw_lo):
        acc = None
        for j in range(CONV_WIDTH):
            start = HIST - (CONV_WIDTH - 1) + j
            term = xbuf[start:start + tb, lo:lo + width] * cw_ref[j:j + 1, cw_lo:cw_lo + width]
            acc = term if acc is None else acc + term
        return _silu(acc)

    lane = lax.broadcasted_iota(jnp.int32, (1, LANES), 1)
    first = 2 * hb * hq
    ba = pltpu.roll(ba_ref[...], (LANES - first) % LANES, 1)
    beta = 1.0 / (1.0 + jnp.exp(-ba))
    alog_row = jnp.zeros((1, LANES), F32)
    dtb_row = jnp.zeros((1, LANES), F32)
    for i in range(2 * hb):
        alog_row = jnp.where(lane == A_LANE + i, alog_ref[first + i], alog_row)
        dtb_row = jnp.where(lane == A_LANE + i, dtb_ref[first + i], dtb_row)
    g = -jnp.exp(alog_row) * _softplus(ba + dtb_row)

    ii = lax.broadcasted_iota(jnp.int32, (tb, tb), 0)
    jj = lax.broadcasted_iota(jnp.int32, (tb, tb), 1)
    lower = ii >= jj
    strict = ii > jj
    diag = ii == jj
    ij_xor = jnp.bitwise_xor(ii, jj)
    tri = jnp.where(lower, 1.0, 0.0).astype(BF16)
    gc = sum(_dot(tri, p) for p in _split_bf16(g, 3))
    gc_t = gc.T
    nw = nw_ref[...]

    heads = range(2 * hb)
    q16, k16, kf, kk, qk = [], [], [], [], []
    for hl in range(hb):
        qc = conv(hl * dh, dh, cwq_ref, hl * dh)
        kc = conv(k_off + hl * dh, dh, cwk_ref, hl * dh)
        q = qc * lax.rsqrt(jnp.sum(qc * qc, axis=-1, keepdims=True) + EPS) * (dh ** -0.5)
        k = kc * lax.rsqrt(jnp.sum(kc * kc, axis=-1, keepdims=True) + EPS)
        kf.append(k)
        q16.append(q.astype(BF16))
        k16.append(k.astype(BF16))
        kk.append(_dot_nt(k16[hl], k16[hl]))
        qk.append(_dot_nt(q16[hl], k16[hl]))

    gcol = [gc[:, A_LANE + hv:A_LANE + hv + 1] for hv in heads]
    grow = [gc_t[A_LANE + hv:A_LANE + hv + 1, :] for hv in heads]
    bcol = [beta[:, BETA_LANE + hv:BETA_LANE + hv + 1] for hv in heads]
    glast = [grow[hv][:, tb - 1:tb] for hv in heads]
    decay = [jnp.where(lower, jnp.exp(gcol[hv] - grow[hv]), 0.0) for hv in heads]

    lmat = [jnp.where(strict, (bcol[hv] * kk[hv // 2]) * decay[hv], 0.0) for hv in heads]
    dinv = [jnp.where(diag, 1.0, jnp.where(ij_xor == 1, -lmat[hv], 0.0)) for hv in heads]
    for level in range(1, GDN_LEVELS):
        in_c = jnp.right_shift(ij_xor, level) == 1
        d16 = [dinv[hv].astype(BF16) for hv in heads]
        cd = [_dot(jnp.where(in_c, lmat[hv], 0.0).astype(BF16), d16[hv]).astype(BF16) for hv in heads]
        dinv = [dinv[hv] - _dot(d16[hv], cd[hv]) for hv in heads]
    r16 = [jnp.where(diag, 0.0, dinv[hv]).astype(BF16) for hv in heads]

    vb = [bcol[hv] * conv(v_off + hv * dh, dh, cwv_ref, hv * dh) for hv in heads]
    kb = [(bcol[hv] * jnp.exp(gcol[hv])) * kf[hv // 2] for hv in heads]
    u = [vb[hv] + _dot(r16[hv], vb[hv].astype(BF16)) for hv in heads]
    w = [kb[hv] + _dot(r16[hv], kb[hv].astype(BF16)) for hv in heads]
    attn = [(qk[hv // 2] * decay[hv]).astype(BF16) for hv in heads]

    s = [s_ref[hv] for hv in heads]
    s16 = [s[hv].astype(BF16) for hv in heads]
    v_new = [u[hv] - _dot(w[hv].astype(BF16), s16[hv]) for hv in heads]
    o = [jnp.exp(gcol[hv]) * _dot(q16[hv // 2], s16[hv]) + _dot(attn[hv], v_new[hv].astype(BF16))
         for hv in heads]
    for hv in heads:
        k_dec = (jnp.exp(glast[hv] - gcol[hv]) * v_new[hv]).astype(BF16)
        s_ref[hv] = s[hv] * jnp.exp(glast[hv]) + _dot_tn(k16[hv // 2], k_dec)
    for hv in heads:
        zs = z_ref[:, hv * dh:(hv + 1) * dh].astype(F32)
        on = o[hv] * lax.rsqrt(jnp.mean(o[hv] * o[hv], axis=-1, keepdims=True) + EPS) * nw
        o_ref[:, hv * dh:(hv + 1) * dh] = (on * _silu(zs)).astype(o_ref.dtype)

    xbuf[0:HIST, :] = xbuf[tb:tb + HIST, :]


def gdn_core(proj, ba, conv_w_t, a_log, dt_bias, norm_w, *, batch, seq):
    n = proj.shape[0]
    dh = GDN_HEAD_DIM
    hb = GDN_HB
    ng = GDN_QK_HEADS // hb
    tpb = seq // GDN_TB
    row = lambda b, h, t: b * tpb + t
    return pl.pallas_call(
        _gdn_kernel,
        out_shape=jax.ShapeDtypeStruct((n, 2 * GDN_QK_HEADS * dh), BF16),
        grid=(batch, ng, tpb),
        in_specs=[
            pl.BlockSpec(memory_space=pltpu.SMEM),
            pl.BlockSpec(memory_space=pltpu.SMEM),
            pl.BlockSpec((GDN_TB, hb * dh), lambda b, h, t: (row(b, h, t), h)),
            pl.BlockSpec((GDN_TB, hb * dh), lambda b, h, t: (row(b, h, t), ng + h)),
            pl.BlockSpec((GDN_TB, 2 * hb * dh), lambda b, h, t: (row(b, h, t), ng + h)),
            pl.BlockSpec((GDN_TB, 2 * hb * dh), lambda b, h, t: (row(b, h, t), 2 * ng + h)),
            pl.BlockSpec((GDN_TB, LANES), lambda b, h, t: (row(b, h, t), 0)),
            pl.BlockSpec((CONV_WIDTH, hb * dh), lambda b, h, t: (0, h)),
            pl.BlockSpec((CONV_WIDTH, hb * dh), lambda b, h, t: (0, ng + h)),
            pl.BlockSpec((CONV_WIDTH, 2 * hb * dh), lambda b, h, t: (0, ng + h)),
            pl.BlockSpec((1, dh), lambda b, h, t: (0, 0)),
        ],
        out_specs=pl.BlockSpec((GDN_TB, 2 * hb * dh), lambda b, h, t: (row(b, h, t), h)),
        scratch_shapes=[
            pltpu.VMEM((GDN_TB + HIST, 4 * hb * dh), F32),
            pltpu.VMEM((2 * hb, dh, dh), F32),
        ],
        compiler_params=_cparams(("parallel", "parallel", "arbitrary")),
        name="gdn_core",
    )(a_log, dt_bias, proj, proj, proj, proj, ba, conv_w_t, conv_w_t, conv_w_t,
      norm_w.reshape(1, dh))


SB_T = 256


def _sb_kernel(q_ref, k_ref, v_ref, o_ref):
    i = pl.program_id(2)
    tb, dh = SB_T, SB_HEAD_DIM
    lane = lax.broadcasted_iota(jnp.int32, (1, 2 * dh), 1)
    head_mask = [lane < dh, lane >= dh]
    zero16 = jnp.zeros((), BF16)
    q = q_ref[...] * jnp.asarray(dh ** -0.5, BF16)
    qh = [jnp.where(m, q, zero16) for m in head_mask]

    ri = lax.broadcasted_iota(jnp.int32, (tb, tb), 0)
    ci = lax.broadcasted_iota(jnp.int32, (tb, tb), 1)
    upper = jnp.where(ri > ci, 1.0, 0.0).astype(BF16)
    causal = ci < ri

    def block(jb, acc, sums, masked):
        s0 = pl.multiple_of(jb * tb, tb)
        kb = k_ref[pl.ds(s0, tb), :]
        vb = v_ref[pl.ds(s0, tb), :]
        new_sums = []
        for h in range(2):
            z2 = _dot_nt(qh[h], kb) * LOG2E
            sp2 = jnp.maximum(z2, 0.0) + jnp.log2(1.0 + jnp.exp2(-jnp.abs(z2)))
            if masked:
                sp2 = jnp.where(causal, sp2, 0.0)
            within = sum(_dot(p, upper) for p in _split_bf16(sp2, 2))
            att = jnp.exp2(z2 - sp2 - within - sums[h])
            if masked:
                att = jnp.where(causal, att, 0.0)
            acc = acc + _dot(att.astype(BF16), jnp.where(head_mask[h], vb, zero16))
            new_sums.append(sums[h] + jnp.sum(sp2, axis=-1, keepdims=True))
        return acc, new_sums

    zeros = jnp.zeros((tb, 1), F32)
    acc, sums = block(i, jnp.zeros((tb, 2 * dh), F32), [zeros, zeros], masked=True)

    def cond(carry):
        jb, _, sum0, sum1 = carry
        alive = jnp.minimum(jnp.min(sum0), jnp.min(sum1)) <= -EXP2_ZERO_BELOW
        return jnp.logical_and(jb >= 0, alive)

    def body(carry):
        jb, acc, sum0, sum1 = carry
        acc, (sum0, sum1) = block(jb, acc, [sum0, sum1], masked=False)
        return jb - 1, acc, sum0, sum1

    _, acc, _, _ = lax.while_loop(cond, body, (i - 1, acc, sums[0], sums[1]))
    o_ref[...] = acc.astype(o_ref.dtype)


def sb_attention(qkv, *, batch, seq):
    n = qkv.shape[1]
    npair = 8
    qpb = seq // SB_T
    return pl.pallas_call(
        _sb_kernel,
        out_shape=jax.ShapeDtypeStruct((n, npair * LANES), BF16),
        grid=(batch, npair, qpb),
        in_specs=[
            pl.BlockSpec((None, SB_T, LANES), lambda b, p, i: (p, b * qpb + i, 0)),
            pl.BlockSpec((None, seq, LANES), lambda b, p, i: (npair + p, b, 0)),
            pl.BlockSpec((None, seq, LANES), lambda b, p, i: (2 * npair + p, b, 0)),
        ],
        out_specs=pl.BlockSpec((SB_T, LANES), lambda b, p, i: (b * qpb + i, p)),
        compiler_params=_cparams(("parallel", "parallel", "arbitrary")),
        name="sb_attention",
    )(qkv, qkv, qkv)


def kernel(x, gdn_w_in, gdn_conv_w, gdn_a_log, gdn_dt_bias, gdn_norm_w, gdn_w_out, sb_w_qkv, sb_w_o,
           mix_norm_w, ffn_norm_w, ffn_w_gate, ffn_w_up, ffn_w_down, final_norm_w):
    batch, seq, d = x.shape
    n = batch * seq
    depth = mix_norm_w.shape[0]
    n_proj = 6 * d
    xf = x.reshape(n, d)
    for i in range(depth):
        j = i // 2
        if i % 2 == 0:
            w_in = gdn_w_in[j]
            w_main = w_in[:, :n_proj].astype(BF16)
            w_ba = jnp.pad(w_in[:, n_proj:], ((0, 0), (0, LANES - (w_in.shape[1] - n_proj)))).astype(BF16)
            proj = norm_matmul(xf, mix_norm_w[i], w_main, tm=512, tn=512, out_dtype=F32, tile_major=False)
            ba = norm_matmul(xf, mix_norm_w[i], w_ba, tm=512, tn=LANES, out_dtype=F32, tile_major=False)
            og = gdn_core(proj, ba, gdn_conv_w[j].T, gdn_a_log[j], gdn_dt_bias[j], gdn_norm_w[j],
                          batch=batch, seq=seq)
            xf = matmul_residual(og, gdn_w_out[j].astype(BF16), xf, tm=512)
        else:
            qkv = norm_matmul(xf, mix_norm_w[i], sb_w_qkv[j].astype(BF16), tm=512, tn=LANES,
                              out_dtype=BF16, tile_major=True)
            o = sb_attention(qkv, batch=batch, seq=seq)
            xf = matmul_residual(o, sb_w_o[j].astype(BF16), xf, tm=512)
        xf = ffn_residual(xf, ffn_norm_w[i], ffn_w_gate[i].astype(BF16), ffn_w_up[i].astype(BF16),
                          ffn_w_down[i].astype(BF16), final_norm_w, tm=512, tf=256,
                          final_norm=(i == depth - 1))
    return xf.reshape(batch, seq, d)
```

```python
import functools
import math

import jax
import jax.numpy as jnp
from jax import lax
from jax.experimental import pallas as pl
from jax.experimental.pallas import tpu as pltpu

F32 = jnp.float32
BF16 = jnp.bfloat16

EPS = 1e-6
GDN_HEAD_DIM = 128
SB_HEAD_DIM = 64
CONV_WIDTH = 4
LANES = 128
VMEM_LIMIT_BYTES = 48 * 1024 * 1024
LOG2E = 1.4426950408889634
EXP2_ZERO_BELOW = -104.0 * LOG2E


def _cparams(sem):
    return pltpu.CompilerParams(dimension_semantics=sem, vmem_limit_bytes=VMEM_LIMIT_BYTES)


def _silu(x):
    return x * (1.0 / (1.0 + jnp.exp(-x)))


def _softplus(x):
    return jnp.maximum(x, 0.0) + jnp.log(1.0 + jnp.exp(-jnp.abs(x)))


def _dot(a, b):
    return jnp.dot(a, b, preferred_element_type=F32)


def _dot_nt(a, b):
    return lax.dot_general(a, b, (((1,), (1,)), ((), ())), preferred_element_type=F32)


def _dot_tn(a, b):
    return lax.dot_general(a, b, (((0,), (0,)), ((), ())), preferred_element_type=F32)


def _split_bf16(x, terms):
    parts = []
    r = x
    for _ in range(terms):
        p = r.astype(BF16)
        parts.append(p)
        r = r - p.astype(F32)
    return parts


MXU_N = 256


def _resident(block_shape):
    zeros = (0,) * len(block_shape)
    return pl.BlockSpec(block_shape, lambda i: zeros, pipeline_mode=pl.Buffered(1))


def _norm_matmul_kernel(x_ref, nw_ref, w_ref, *rest, tn, tile_major):
    if len(rest) == 3:
        w2_ref, o_ref, o2_ref = rest
    else:
        (o_ref,) = rest
        w2_ref = o2_ref = None
    x = x_ref[...]
    y = x * lax.rsqrt(jnp.mean(x * x, axis=-1, keepdims=True) + EPS)
    h = (y * nw_ref[...]).astype(BF16)
    n_out = w_ref.shape[1]
    for c0 in range(0, n_out, tn):
        r = _dot(h, w_ref[:, c0:c0 + tn]).astype(o_ref.dtype)
        if tile_major:
            for l0 in range(0, tn, LANES):
                o_ref[(c0 + l0) // LANES] = r[:, l0:l0 + LANES]
        else:
            o_ref[:, c0:c0 + tn] = r
    if w2_ref is not None:
        o2_ref[...] = _dot(h, w2_ref[...])


def norm_matmul(x, norm_w, w, w2=None, *, tm, tn, out_dtype, tile_major):
    n, d = x.shape
    n_out = w.shape[1]
    if tile_major:
        out_shape = [jax.ShapeDtypeStruct((n_out // LANES, n, LANES), out_dtype)]
        out_specs = [pl.BlockSpec((n_out // LANES, tm, LANES), lambda i: (0, i, 0))]
    else:
        out_shape = [jax.ShapeDtypeStruct((n, n_out), out_dtype)]
        out_specs = [pl.BlockSpec((tm, n_out), lambda i: (i, 0))]
    operands = [x, norm_w.reshape(1, d), w]
    in_specs = [pl.BlockSpec((tm, d), lambda i: (i, 0)), _resident((1, d)), _resident((d, n_out))]
    if w2 is not None:
        n2 = w2.shape[1]
        operands.append(w2)
        in_specs.append(_resident((d, n2)))
        out_shape.append(jax.ShapeDtypeStruct((n, n2), F32))
        out_specs.append(pl.BlockSpec((tm, n2), lambda i: (i, 0)))
    out = pl.pallas_call(
        functools.partial(_norm_matmul_kernel, tn=tn, tile_major=tile_major),
        out_shape=out_shape,
        grid=(n // tm,),
        in_specs=in_specs,
        out_specs=out_specs,
        compiler_params=_cparams(("parallel",)),
        name="norm_matmul",
    )(*operands)
    return out if w2 is not None else out[0]


def _matmul_residual_kernel(a_ref, w_ref, x_ref, o_ref):
    o_ref[...] = x_ref[...] + _dot(a_ref[...], w_ref[...])


def matmul_residual(a, w, x, *, tm):
    n, k = a.shape
    d = w.shape[1]
    return pl.pallas_call(
        _matmul_residual_kernel,
        out_shape=jax.ShapeDtypeStruct((n, d), F32),
        grid=(n // tm,),
        in_specs=[
            pl.BlockSpec((tm, k), lambda i: (i, 0)),
            pl.BlockSpec((k, d), lambda i: (0, 0)),
            pl.BlockSpec((tm, d), lambda i: (i, 0)),
        ],
        out_specs=pl.BlockSpec((tm, d), lambda i: (i, 0)),
        compiler_params=_cparams(("parallel",)),
        name="matmul_residual",
    )(a, w, x)


def _ffn_kernel(x_ref, nw_ref, wg_ref, wu_ref, wd_ref, fw_ref, o_ref, *, tf, final_norm):
    x = x_ref[...]
    y = x * lax.rsqrt(jnp.mean(x * x, axis=-1, keepdims=True) + EPS)
    h = (y * nw_ref[...]).astype(BF16)
    f = wg_ref.shape[1]
    acc = x
    for c0 in range(0, f, tf):
        g = _dot(h, wg_ref[:, c0:c0 + tf])
        u = _dot(h, wu_ref[:, c0:c0 + tf])
        a = (_silu(g) * u).astype(BF16)
        acc = acc + _dot(a, wd_ref[c0:c0 + tf, :])
    if final_norm:
        acc = acc * lax.rsqrt(jnp.mean(acc * acc, axis=-1, keepdims=True) + EPS) * fw_ref[...]
    o_ref[...] = acc


def ffn_residual(x, norm_w, wg, wu, wd, final_w, *, tm, tf, final_norm):
    n, d = x.shape
    f = wg.shape[1]
    return pl.pallas_call(
        functools.partial(_ffn_kernel, tf=tf, final_norm=final_norm),
        out_shape=jax.ShapeDtypeStruct((n, d), F32),
        grid=(n // tm,),
        in_specs=[
            pl.BlockSpec((tm, d), lambda i: (i, 0)),
            _resident((1, d)),
            _resident((d, f)),
            _resident((d, f)),
            _resident((f, d)),
            _resident((1, d)),
        ],
        out_specs=pl.BlockSpec((tm, d), lambda i: (i, 0)),
        compiler_params=_cparams(("parallel",)),
        name="ffn_residual",
    )(x, norm_w.reshape(1, d), wg, wu, wd, final_w.reshape(1, d))


GDN_TB = 256
GDN_LEVELS = GDN_TB.bit_length() - 1
GDN_HB = 2
GDN_QK_HEADS = 8
HIST = 8
BETA_LANE = 0
A_LANE = 16


def _gdn_kernel(alog_ref, dtb_ref, q_ref, k_ref, v_ref, z_ref, ba_ref, cwq_ref, cwk_ref, cwv_ref,
                nw_ref, o_ref, xbuf, s_ref):
    hq = pl.program_id(1)
    t = pl.program_id(2)
    tb = GDN_TB
    dh = GDN_HEAD_DIM
    hb = GDN_HB
    k_off = hb * dh
    v_off = 2 * hb * dh

    @pl.when(t == 0)
    def _():
        xbuf[0:HIST, :] = jnp.zeros((HIST, 4 * hb * dh), F32)
        s_ref[...] = jnp.zeros_like(s_ref)

    xbuf[HIST:HIST + tb, 0:k_off] = q_ref[...].astype(F32)
    xbuf[HIST:HIST + tb, k_off:v_off] = k_ref[...].astype(F32)
    xbuf[HIST:HIST + tb, v_off:] = v_ref[...].astype(F32)

    def conv(lo, width, cw_ref, cw_lo):
        acc = None
        for j in range(CONV_WIDTH):
            start = HIST - (CONV_WIDTH - 1) + j
            term = xbuf[start:start + tb, lo:lo + width] * cw_ref[j:j + 1, cw_lo:cw_lo + width]
            acc = term if acc is None else acc + term
        return _silu(acc)

    lane = lax.broadcasted_iota(jnp.int32, (1, LANES), 1)
    first = 2 * hb * hq
    ba = pltpu.roll(ba_ref[...], (LANES - first) % LANES, 1)
    beta = 1.0 / (1.0 + jnp.exp(-ba))
    alog_row = jnp.zeros((1, LANES), F32)
    dtb_row = jnp.zeros((1, LANES), F32)
    for i in range(2 * hb):
        alog_row = jnp.where(lane == A_LANE + i, alog_ref[first + i], alog_row)
        dtb_row = jnp.where(lane == A_LANE + i, dtb_ref[first + i], dtb_row)
    g = -jnp.exp(alog_row) * _softplus(ba + dtb_row)

    ii = lax.broadcasted_iota(jnp.int32, (tb, tb), 0)
    jj = lax.broadcasted_iota(jnp.int32, (tb, tb), 1)
    lower = ii >= jj
    strict = ii > jj
    diag = ii == jj
    ij_xor = jnp.bitwise_xor(ii, jj)
    tri = jnp.where(lower, 1.0, 0.0).astype(BF16)
    gc = sum(_dot(tri, p) for p in _split_bf16(g, 3))
    gc_t = gc.T
    nw = nw_ref[...]

    heads = range(2 * hb)
    q16, k16, kf, kk, qk = [], [], [], [], []
    for hl in range(hb):
        qc = conv(hl * dh, dh, cwq_ref, hl * dh)
        kc = conv(k_off + hl * dh, dh, cwk_ref, hl * dh)
        q = qc * lax.rsqrt(jnp.sum(qc * qc, axis=-1, keepdims=True) + EPS) * (dh ** -0.5)
        k = kc * lax.rsqrt(jnp.sum(kc * kc, axis=-1, keepdims=True) + EPS)
        kf.append(k)
        q16.append(q.astype(BF16))
        k16.append(k.astype(BF16))
        kk.append(_dot_nt(k16[hl], k16[hl]))
        qk.append(_dot_nt(q16[hl], k16[hl]))

    gcol = [gc[:, A_LANE + hv:A_LANE + hv + 1] for hv in heads]
    grow = [gc_t[A_LANE + hv:A_LANE + hv + 1, :] for hv in heads]
    bcol = [beta[:, BETA_LANE + hv:BETA_LANE + hv + 1] for hv in heads]
    glast = [grow[hv][:, tb - 1:tb] for hv in heads]
    decay = [jnp.where(lower, jnp.exp(gcol[hv] - grow[hv]), 0.0) for hv in heads]

    lmat = [jnp.where(strict, (bcol[hv] * kk[hv // 2]) * decay[hv], 0.0) for hv in heads]
    dinv = [jnp.where(diag, 1.0, jnp.where(ij_xor == 1, -lmat[hv], 0.0)) for hv in heads]
    for level in range(1, GDN_LEVELS):
        in_c = jnp.right_shift(ij_xor, level) == 1
        d16 = [dinv[hv].astype(BF16) for hv in heads]
        cd = [_dot(jnp.where(in_c, lmat[hv], 0.0).astype(BF16), d16[hv]).astype(BF16) for hv in heads]
        dinv = [dinv[hv] - _dot(d16[hv], cd[hv]) for hv in heads]
    r16 = [jnp.where(diag, 0.0, dinv[hv]).astype(BF16) for hv in heads]

    vb = [bcol[hv] * conv(v_off + hv * dh, dh, cwv_ref, hv * dh) for hv in heads]
    kb = [(bcol[hv] * jnp.exp(gcol[hv])) * kf[hv // 2] for hv in heads]
    u = [vb[hv] + _dot(r16[hv], vb[hv].astype(BF16)) for hv in heads]
    w = [kb[hv] + _dot(r16[hv], kb[hv].astype(BF16)) for hv in heads]
    attn = [(qk[hv // 2] * decay[hv]).astype(BF16) for hv in heads]

    s = [s_ref[hv] for hv in heads]
    s16 = [s[hv].astype(BF16) for hv in heads]
    v_new = [u[hv] - _dot(w[hv].astype(BF16), s16[hv]) for hv in heads]
    o = [jnp.exp(gcol[hv]) * _dot(q16[hv // 2], s16[hv]) + _dot(attn[hv], v_new[hv].astype(BF16))
         for hv in heads]
    for hv in heads:
        k_dec = (jnp.exp(glast[hv] - gcol[hv]) * v_new[hv]).astype(BF16)
        s_ref[hv] = s[hv] * jnp.exp(glast[hv]) + _dot_tn(k16[hv // 2], k_dec)
    for hv in heads:
        zs = z_ref[:, hv * dh:(hv + 1) * dh].astype(F32)
        on = o[hv] * lax.rsqrt(jnp.mean(o[hv] * o[hv], axis=-1, keepdims=True) + EPS) * nw
        o_ref[:, hv * dh:(hv + 1) * dh] = (on * _silu(zs)).astype(o_ref.dtype)

    xbuf[0:HIST, :] = xbuf[tb:tb + HIST, :]


def gdn_core(proj, ba, conv_w_t, a_log, dt_bias, norm_w, *, batch, seq):
    n = proj.shape[0]
    dh = GDN_HEAD_DIM
    hb = GDN_HB
    ng = GDN_QK_HEADS // hb
    tpb = seq // GDN_TB
    row = lambda b, h, t: b * tpb + t
    return pl.pallas_call(
        _gdn_kernel,
        out_shape=jax.ShapeDtypeStruct((n, 2 * GDN_QK_HEADS * dh), BF16),
        grid=(batch, ng, tpb),
        in_specs=[
            pl.BlockSpec(memory_space=pltpu.SMEM),
            pl.BlockSpec(memory_space=pltpu.SMEM),
            pl.BlockSpec((GDN_TB, hb * dh), lambda b, h, t: (row(b, h, t), h)),
            pl.BlockSpec((GDN_TB, hb * dh), lambda b, h, t: (row(b, h, t), ng + h)),
            pl.BlockSpec((GDN_TB, 2 * hb * dh), lambda b, h, t: (row(b, h, t), ng + h)),
            pl.BlockSpec((GDN_TB, 2 * hb * dh), lambda b, h, t: (row(b, h, t), 2 * ng + h)),
            pl.BlockSpec((GDN_TB, LANES), lambda b, h, t: (row(b, h, t), 0)),
            pl.BlockSpec((CONV_WIDTH, hb * dh), lambda b, h, t: (0, h)),
            pl.BlockSpec((CONV_WIDTH, hb * dh), lambda b, h, t: (0, ng + h)),
            pl.BlockSpec((CONV_WIDTH, 2 * hb * dh), lambda b, h, t: (0, ng + h)),
            pl.BlockSpec((1, dh), lambda b, h, t: (0, 0)),
        ],
        out_specs=pl.BlockSpec((GDN_TB, 2 * hb * dh), lambda b, h, t: (row(b, h, t), h)),
        scratch_shapes=[
            pltpu.VMEM((GDN_TB + HIST, 4 * hb * dh), F32),
            pltpu.VMEM((2 * hb, dh, dh), F32),
        ],
        compiler_params=_cparams(("parallel", "parallel", "arbitrary")),
        name="gdn_core",
    )(a_log, dt_bias, proj, proj, proj, proj, ba, conv_w_t, conv_w_t, conv_w_t,
      norm_w.reshape(1, dh))


SB_T = 256


def _sb_kernel(q_ref, k_ref, v_ref, o_ref):
    i = pl.program_id(2)
    tb, dh = SB_T, SB_HEAD_DIM
    lane = lax.broadcasted_iota(jnp.int32, (1, 2 * dh), 1)
    head_mask = [lane < dh, lane >= dh]
    zero16 = jnp.zeros((), BF16)
    q = q_ref[...] * jnp.asarray(dh ** -0.5, BF16)
    qh = [jnp.where(m, q, zero16) for m in head_mask]

    ri = lax.broadcasted_iota(jnp.int32, (tb, tb), 0)
    ci = lax.broadcasted_iota(jnp.int32, (tb, tb), 1)
    upper = jnp.where(ri > ci, 1.0, 0.0).astype(BF16)
    causal = ci < ri

    def block(jb, acc, sums, masked):
        s0 = pl.multiple_of(jb * tb, tb)
        kb = k_ref[pl.ds(s0, tb), :]
        vb = v_ref[pl.ds(s0, tb), :]
        hs = range(2)
        z2 = [_dot_nt(qh[h], kb) * LOG2E for h in hs]
        sp2 = [jnp.maximum(z2[h], 0.0) + jnp.log2(1.0 + jnp.exp2(-jnp.abs(z2[h]))) for h in hs]
        if masked:
            sp2 = [jnp.where(causal, sp2[h], 0.0) for h in hs]
        parts = [_split_bf16(sp2[h], 2) for h in hs]
        within = [_dot(parts[h][0], upper) + _dot(parts[h][1], upper) for h in hs]
        att = [jnp.exp2(z2[h] - sp2[h] - within[h] - sums[h]) for h in hs]
        if masked:
            att = [jnp.where(causal, att[h], 0.0) for h in hs]
        for h in hs:
            acc = acc + _dot(att[h].astype(BF16), jnp.where(head_mask[h], vb, zero16))
        new_sums = [sums[h] + jnp.sum(sp2[h], axis=-1, keepdims=True) for h in hs]
        return acc, new_sums

    zeros = jnp.zeros((tb, 1), F32)
    acc, sums = block(i, jnp.zeros((tb, 2 * dh), F32), [zeros, zeros], masked=True)

    def cond(carry):
        jb, _, sum0, sum1 = carry
        alive = jnp.minimum(jnp.min(sum0), jnp.min(sum1)) <= -EXP2_ZERO_BELOW
        return jnp.logical_and(jb >= 0, alive)

    def body(carry):
        jb, acc, sum0, sum1 = carry
        acc, (sum0, sum1) = block(jb, acc, [sum0, sum1], masked=False)
        return jb - 1, acc, sum0, sum1

    _, acc, _, _ = lax.while_loop(cond, body, (i - 1, acc, sums[0], sums[1]))
    o_ref[...] = acc.astype(o_ref.dtype)


def sb_attention(qkv, *, batch, seq):
    n = qkv.shape[1]
    npair = 8
    qpb = seq // SB_T
    return pl.pallas_call(
        _sb_kernel,
        out_shape=jax.ShapeDtypeStruct((n, npair * LANES), BF16),
        grid=(batch, npair, qpb),
        in_specs=[
            pl.BlockSpec((None, SB_T, LANES), lambda b, p, i: (p, b * qpb + i, 0)),
            pl.BlockSpec((None, seq, LANES), lambda b, p, i: (npair + p, b, 0)),
            pl.BlockSpec((None, seq, LANES), lambda b, p, i: (2 * npair + p, b, 0)),
        ],
        out_specs=pl.BlockSpec((SB_T, LANES), lambda b, p, i: (b * qpb + i, p)),
        compiler_params=_cparams(("parallel", "parallel", "arbitrary")),
        name="sb_attention",
    )(qkv, qkv, qkv)


def kernel(x, gdn_w_in, gdn_conv_w, gdn_a_log, gdn_dt_bias, gdn_norm_w, gdn_w_out, sb_w_qkv, sb_w_o,
           mix_norm_w, ffn_norm_w, ffn_w_gate, ffn_w_up, ffn_w_down, final_norm_w):
    batch, seq, d = x.shape
    n = batch * seq
    depth = mix_norm_w.shape[0]
    n_proj = 6 * d
    xf = x.reshape(n, d)
    for i in range(depth):
        j = i // 2
        if i % 2 == 0:
            w_in = gdn_w_in[j]
            w_main = w_in[:, :n_proj].astype(BF16)
            w_ba = jnp.pad(w_in[:, n_proj:], ((0, 0), (0, LANES - (w_in.shape[1] - n_proj)))).astype(BF16)
            proj, ba = norm_matmul(xf, mix_norm_w[i], w_main, w_ba, tm=512, tn=2 * MXU_N, out_dtype=BF16,
                                   tile_major=False)
            og = gdn_core(proj, ba, gdn_conv_w[j].T, gdn_a_log[j], gdn_dt_bias[j], gdn_norm_w[j],
                          batch=batch, seq=seq)
            xf = matmul_residual(og, gdn_w_out[j].astype(BF16), xf, tm=512)
        else:
            qkv = norm_matmul(xf, mix_norm_w[i], sb_w_qkv[j].astype(BF16), tm=512, tn=2 * MXU_N,
                              out_dtype=BF16, tile_major=True)
            o = sb_attention(qkv, batch=batch, seq=seq)
            xf = matmul_residual(o, sb_w_o[j].astype(BF16), xf, tm=512)
        xf = ffn_residual(xf, ffn_norm_w[i], ffn_w_gate[i].astype(BF16), ffn_w_up[i].astype(BF16),
                          ffn_w_down[i].astype(BF16), final_norm_w, tm=512, tf=256,
                          final_norm=(i == depth - 1))
    return xf.reshape(batch, seq, d)
```

```python
import functools
import math

import jax
import jax.numpy as jnp
from jax import lax
from jax.experimental import pallas as pl
from jax.experimental.pallas import tpu as pltpu

F32 = jnp.float32
BF16 = jnp.bfloat16

EPS = 1e-6
GDN_HEAD_DIM = 128
SB_HEAD_DIM = 64
CONV_WIDTH = 4
LANES = 128
VMEM_LIMIT_BYTES = 48 * 1024 * 1024
EXP_ZERO_BELOW = -104.0


def _cparams(sem):
    return pltpu.CompilerParams(dimension_semantics=sem, vmem_limit_bytes=VMEM_LIMIT_BYTES)


def _silu(x):
    return x * (1.0 / (1.0 + jnp.exp(-x)))


def _softplus(x):
    return jnp.maximum(x, 0.0) + jnp.log(1.0 + jnp.exp(-jnp.abs(x)))


def _dot(a, b):
    return jnp.dot(a, b, preferred_element_type=F32)


def _dot_nt(a, b):
    return lax.dot_general(a, b, (((1,), (1,)), ((), ())), preferred_element_type=F32)


def _dot_tn(a, b):
    return lax.dot_general(a, b, (((0,), (0,)), ((), ())), preferred_element_type=F32)


def _split_bf16(x, terms):
    parts = []
    r = x
    for _ in range(terms):
        p = r.astype(BF16)
        parts.append(p)
        r = r - p.astype(F32)
    return parts


MXU_N = 256


def _resident(block_shape):
    zeros = (0,) * len(block_shape)
    return pl.BlockSpec(block_shape, lambda i: zeros, pipeline_mode=pl.Buffered(1))


def _norm_matmul_kernel(x_ref, nw_ref, w_ref, *rest, tn, tile_major):
    if len(rest) == 3:
        w2_ref, o_ref, o2_ref = rest
    else:
        (o_ref,) = rest
        w2_ref = o2_ref = None
    x = x_ref[...]
    y = x * lax.rsqrt(jnp.mean(x * x, axis=-1, keepdims=True) + EPS)
    h = (y * nw_ref[...]).astype(BF16)
    n_out = w_ref.shape[1]
    for c0 in range(0, n_out, tn):
        r = _dot(h, w_ref[:, c0:c0 + tn]).astype(o_ref.dtype)
        if tile_major:
            for l0 in range(0, tn, LANES):
                o_ref[(c0 + l0) // LANES] = r[:, l0:l0 + LANES]
        else:
            o_ref[:, c0:c0 + tn] = r
    if w2_ref is not None:
        o2_ref[...] = _dot(h, w2_ref[...])


def norm_matmul(x, norm_w, w, w2=None, *, tm, tn, out_dtype, tile_major):
    n, d = x.shape
    n_out = w.shape[1]
    if tile_major:
        out_shape = [jax.ShapeDtypeStruct((n_out // LANES, n, LANES), out_dtype)]
        out_specs = [pl.BlockSpec((n_out // LANES, tm, LANES), lambda i: (0, i, 0))]
    else:
        out_shape = [jax.ShapeDtypeStruct((n, n_out), out_dtype)]
        out_specs = [pl.BlockSpec((tm, n_out), lambda i: (i, 0))]
    operands = [x, norm_w.reshape(1, d), w]
    in_specs = [pl.BlockSpec((tm, d), lambda i: (i, 0)), _resident((1, d)), _resident((d, n_out))]
    if w2 is not None:
        n2 = w2.shape[1]
        operands.append(w2)
        in_specs.append(_resident((d, n2)))
        out_shape.append(jax.ShapeDtypeStruct((n, n2), F32))
        out_specs.append(pl.BlockSpec((tm, n2), lambda i: (i, 0)))
    out = pl.pallas_call(
        functools.partial(_norm_matmul_kernel, tn=tn, tile_major=tile_major),
        out_shape=out_shape,
        grid=(n // tm,),
        in_specs=in_specs,
        out_specs=out_specs,
        compiler_params=_cparams(("parallel",)),
        name="norm_matmul",
    )(*operands)
    return out if w2 is not None else out[0]


def _matmul_residual_kernel(a_ref, w_ref, x_ref, o_ref):
    o_ref[...] = x_ref[...] + _dot(a_ref[...], w_ref[...])


def matmul_residual(a, w, x, *, tm):
    n, k = a.shape
    d = w.shape[1]
    return pl.pallas_call(
        _matmul_residual_kernel,
        out_shape=jax.ShapeDtypeStruct((n, d), F32),
        grid=(n // tm,),
        in_specs=[
            pl.BlockSpec((tm, k), lambda i: (i, 0)),
            pl.BlockSpec((k, d), lambda i: (0, 0)),
            pl.BlockSpec((tm, d), lambda i: (i, 0)),
        ],
        out_specs=pl.BlockSpec((tm, d), lambda i: (i, 0)),
        compiler_params=_cparams(("parallel",)),
        name="matmul_residual",
    )(a, w, x)


def _ffn_kernel(x_ref, nw_ref, wg_ref, wu_ref, wd_ref, fw_ref, o_ref, *, tf, final_norm):
    x = x_ref[...]
    y = x * lax.rsqrt(jnp.mean(x * x, axis=-1, keepdims=True) + EPS)
    h = (y * nw_ref[...]).astype(BF16)
    f = wg_ref.shape[1]
    acc = x
    for c0 in range(0, f, tf):
        g = _dot(h, wg_ref[:, c0:c0 + tf])
        u = _dot(h, wu_ref[:, c0:c0 + tf])
        a = (_silu(g) * u).astype(BF16)
        acc = acc + _dot(a, wd_ref[c0:c0 + tf, :])
    if final_norm:
        acc = acc * lax.rsqrt(jnp.mean(acc * acc, axis=-1, keepdims=True) + EPS) * fw_ref[...]
    o_ref[...] = acc


def ffn_residual(x, norm_w, wg, wu, wd, final_w, *, tm, tf, final_norm):
    n, d = x.shape
    f = wg.shape[1]
    return pl.pallas_call(
        functools.partial(_ffn_kernel, tf=tf, final_norm=final_norm),
        out_shape=jax.ShapeDtypeStruct((n, d), F32),
        grid=(n // tm,),
        in_specs=[
            pl.BlockSpec((tm, d), lambda i: (i, 0)),
            _resident((1, d)),
            _resident((d, f)),
            _resident((d, f)),
            _resident((f, d)),
            _resident((1, d)),
        ],
        out_specs=pl.BlockSpec((tm, d), lambda i: (i, 0)),
        compiler_params=_cparams(("parallel",)),
        name="ffn_residual",
    )(x, norm_w.reshape(1, d), wg, wu, wd, final_w.reshape(1, d))


GDN_TB = 256
GDN_LEVELS = GDN_TB.bit_length() - 1
GDN_HB = 2
GDN_QK_HEADS = 8
HIST = 8
BETA_LANE = 0
A_LANE = 16


def _gdn_kernel(alog_ref, dtb_ref, q_ref, k_ref, v_ref, z_ref, ba_ref, cwq_ref, cwk_ref, cwv_ref,
                nw_ref, o_ref, xbuf, s_ref):
    hq = pl.program_id(1)
    t = pl.program_id(2)
    tb = GDN_TB
    dh = GDN_HEAD_DIM
    hb = GDN_HB
    k_off = hb * dh
    v_off = 2 * hb * dh

    @pl.when(t == 0)
    def _():
        xbuf[0:HIST, :] = jnp.zeros((HIST, 4 * hb * dh), F32)
        s_ref[...] = jnp.zeros_like(s_ref)

    xbuf[HIST:HIST + tb, 0:k_off] = q_ref[...].astype(F32)
    xbuf[HIST:HIST + tb, k_off:v_off] = k_ref[...].astype(F32)
    xbuf[HIST:HIST + tb, v_off:] = v_ref[...].astype(F32)

    def conv(lo, width, cw_ref, cw_lo):
        acc = None
        for j in range(CONV_WIDTH):
            start = HIST - (CONV_WIDTH - 1) + j
            term = xbuf[start:start + tb, lo:lo + width] * cw_ref[j:j + 1, cw_lo:cw_lo + width]
            acc = term if acc is None else acc + term
        return _silu(acc)

    lane = lax.broadcasted_iota(jnp.int32, (1, LANES), 1)
    first = 2 * hb * hq
    ba = pltpu.roll(ba_ref[...], (LANES - first) % LANES, 1)
    beta = 1.0 / (1.0 + jnp.exp(-ba))
    alog_row = jnp.zeros((1, LANES), F32)
    dtb_row = jnp.zeros((1, LANES), F32)
    for i in range(2 * hb):
        alog_row = jnp.where(lane == A_LANE + i, alog_ref[first + i], alog_row)
        dtb_row = jnp.where(lane == A_LANE + i, dtb_ref[first + i], dtb_row)
    g = -jnp.exp(alog_row) * _softplus(ba + dtb_row)

    ii = lax.broadcasted_iota(jnp.int32, (tb, tb), 0)
    jj = lax.broadcasted_iota(jnp.int32, (tb, tb), 1)
    lower = ii >= jj
    strict = ii > jj
    diag = ii == jj
    ij_xor = jnp.bitwise_xor(ii, jj)
    tri = jnp.where(lower, 1.0, 0.0).astype(BF16)
    gc = sum(_dot(tri, p) for p in _split_bf16(g, 3))
    gc_t = gc.T
    nw = nw_ref[...]

    heads = range(2 * hb)
    q16, k16, kf, kk, qk = [], [], [], [], []
    for hl in range(hb):
        qc = conv(hl * dh, dh, cwq_ref, hl * dh)
        kc = conv(k_off + hl * dh, dh, cwk_ref, hl * dh)
        q = qc * lax.rsqrt(jnp.sum(qc * qc, axis=-1, keepdims=True) + EPS) * (dh ** -0.5)
        k = kc * lax.rsqrt(jnp.sum(kc * kc, axis=-1, keepdims=True) + EPS)
        kf.append(k)
        q16.append(q.astype(BF16))
        k16.append(k.astype(BF16))
        kk.append(jnp.where(strict, _dot_nt(k16[hl], k16[hl]), 0.0))
        qk.append(_dot_nt(q16[hl], k16[hl]))

    gcol = [gc[:, A_LANE + hv:A_LANE + hv + 1] for hv in heads]
    grow = [gc_t[A_LANE + hv:A_LANE + hv + 1, :] for hv in heads]
    bcol = [beta[:, BETA_LANE + hv:BETA_LANE + hv + 1] for hv in heads]
    glast = [grow[hv][:, tb - 1:tb] for hv in heads]
    decay = [jnp.where(lower, jnp.exp(gcol[hv] - grow[hv]), 0.0) for hv in heads]

    lmat = [(bcol[hv] * kk[hv // 2]) * decay[hv] for hv in heads]
    l16 = [lmat[hv].astype(BF16) for hv in heads]
    top_bit = (jnp.int32(31) - lax.clz(ij_xor))
    eye = jnp.where(diag, 1.0, 0.0)
    dinv = [eye - jnp.where(top_bit == 0, lmat[hv], 0.0) for hv in heads]
    for level in range(1, GDN_LEVELS):
        in_c = top_bit == level
        d16 = [dinv[hv].astype(BF16) for hv in heads]
        ld = [_dot(l16[hv], d16[hv]).astype(BF16) for hv in heads]
        dinv = [dinv[hv] - jnp.where(in_c, _dot(d16[hv], ld[hv]), 0.0) for hv in heads]
    r16 = [(dinv[hv] - eye).astype(BF16) for hv in heads]

    vb = [bcol[hv] * conv(v_off + hv * dh, dh, cwv_ref, hv * dh) for hv in heads]
    kb = [(bcol[hv] * jnp.exp(gcol[hv])) * kf[hv // 2] for hv in heads]
    ruw = [_dot(r16[hv], jnp.concatenate([vb[hv].astype(BF16), kb[hv].astype(BF16)], axis=1))
           for hv in heads]
    u = [vb[hv] + ruw[hv][:, :dh] for hv in heads]
    w = [kb[hv] + ruw[hv][:, dh:] for hv in heads]
    attn = [(qk[hv // 2] * decay[hv]).astype(BF16) for hv in heads]

    s = [s_ref[hv] for hv in heads]
    s16 = [s[hv].astype(BF16) for hv in heads]
    v_new = [u[hv] - _dot(w[hv].astype(BF16), s16[hv]) for hv in heads]
    o = [jnp.exp(gcol[hv]) * _dot(q16[hv // 2], s16[hv]) + _dot(attn[hv], v_new[hv].astype(BF16))
         for hv in heads]
    for hv in heads:
        k_dec = (jnp.exp(glast[hv] - gcol[hv]) * v_new[hv]).astype(BF16)
        s_ref[hv] = s[hv] * jnp.exp(glast[hv]) + _dot_tn(k16[hv // 2], k_dec)
    for hv in heads:
        zs = z_ref[:, hv * dh:(hv + 1) * dh].astype(F32)
        on = o[hv] * lax.rsqrt(jnp.mean(o[hv] * o[hv], axis=-1, keepdims=True) + EPS) * nw
        o_ref[:, hv * dh:(hv + 1) * dh] = (on * _silu(zs)).astype(o_ref.dtype)

    xbuf[0:HIST, :] = xbuf[tb:tb + HIST, :]


def gdn_core(proj, ba, conv_w_t, a_log, dt_bias, norm_w, *, batch, seq):
    n = proj.shape[0]
    dh = GDN_HEAD_DIM
    hb = GDN_HB
    ng = GDN_QK_HEADS // hb
    tpb = seq // GDN_TB
    row = lambda b, h, t: b * tpb + t
    return pl.pallas_call(
        _gdn_kernel,
        out_shape=jax.ShapeDtypeStruct((n, 2 * GDN_QK_HEADS * dh), BF16),
        grid=(batch, ng, tpb),
        in_specs=[
            pl.BlockSpec(memory_space=pltpu.SMEM),
            pl.BlockSpec(memory_space=pltpu.SMEM),
            pl.BlockSpec((GDN_TB, hb * dh), lambda b, h, t: (row(b, h, t), h)),
            pl.BlockSpec((GDN_TB, hb * dh), lambda b, h, t: (row(b, h, t), ng + h)),
            pl.BlockSpec((GDN_TB, 2 * hb * dh), lambda b, h, t: (row(b, h, t), ng + h)),
            pl.BlockSpec((GDN_TB, 2 * hb * dh), lambda b, h, t: (row(b, h, t), 2 * ng + h)),
            pl.BlockSpec((GDN_TB, LANES), lambda b, h, t: (row(b, h, t), 0)),
            pl.BlockSpec((CONV_WIDTH, hb * dh), lambda b, h, t: (0, h)),
            pl.BlockSpec((CONV_WIDTH, hb * dh), lambda b, h, t: (0, ng + h)),
            pl.BlockSpec((CONV_WIDTH, 2 * hb * dh), lambda b, h, t: (0, ng + h)),
            pl.BlockSpec((1, dh), lambda b, h, t: (0, 0)),
        ],
        out_specs=pl.BlockSpec((GDN_TB, 2 * hb * dh), lambda b, h, t: (row(b, h, t), h)),
        scratch_shapes=[
            pltpu.VMEM((GDN_TB + HIST, 4 * hb * dh), F32),
            pltpu.VMEM((2 * hb, dh, dh), F32),
        ],
        compiler_params=_cparams(("parallel", "parallel", "arbitrary")),
        name="gdn_core",
    )(a_log, dt_bias, proj, proj, proj, proj, ba, conv_w_t, conv_w_t, conv_w_t,
      norm_w.reshape(1, dh))


SB_T = 256


def _sb_kernel(q_ref, k_ref, v_ref, o_ref):
    i = pl.program_id(2)
    tb, dh = SB_T, SB_HEAD_DIM
    lane = lax.broadcasted_iota(jnp.int32, (1, 2 * dh), 1)
    head_mask = [lane < dh, lane >= dh]
    zero16 = jnp.zeros((), BF16)
    q = q_ref[...] * jnp.asarray(dh ** -0.5, BF16)
    qh = [jnp.where(m, q, zero16) for m in head_mask]

    ri = lax.broadcasted_iota(jnp.int32, (tb, tb), 0)
    ci = lax.broadcasted_iota(jnp.int32, (tb, tb), 1)
    upper = jnp.where(ri > ci, 1.0, 0.0).astype(BF16)
    upper2 = jnp.concatenate([upper, upper], axis=0)
    causal = ci < ri

    def visit(blocks, acc, sums):
        pairs = [(b, h) for b in range(len(blocks)) for h in range(2)]
        kbs, vbs = [], []
        for jb, _, _ in blocks:
            s0 = pl.multiple_of(jb * tb, tb)
            kbs.append(k_ref[pl.ds(s0, tb), :])
            vbs.append(v_ref[pl.ds(s0, tb), :])
        z2 = {p: _dot_nt(qh[p[1]], kbs[p[0]]) for p in pairs}
        sp2 = {p: _softplus(z2[p]) for p in pairs}
        for p in pairs:
            _, diagonal, exists = blocks[p[0]]
            if diagonal:
                sp2[p] = jnp.where(causal, sp2[p], 0.0)
            if exists is not None:
                sp2[p] = jnp.where(exists, sp2[p], 0.0)
        within = {}
        for p in pairs:
            hi, lo = _split_bf16(sp2[p], 2)
            within[p] = _dot(jnp.concatenate([hi, lo], axis=1), upper2)
        total = {p: jnp.sum(sp2[p], axis=-1, keepdims=True) for p in pairs}
        sums = list(sums)
        att = {}
        for p in pairs:
            att[p] = jnp.exp(z2[p] - sp2[p] - within[p] - sums[p[1]])
            sums[p[1]] = sums[p[1]] + total[p]
            _, diagonal, exists = blocks[p[0]]
            if diagonal:
                att[p] = jnp.where(causal, att[p], 0.0)
            if exists is not None:
                att[p] = jnp.where(exists, att[p], 0.0)
        for p in pairs:
            acc = acc + _dot(att[p].astype(BF16), jnp.where(head_mask[p[1]], vbs[p[0]], zero16))
        return acc, sums

    zeros = jnp.zeros((tb, 1), F32)
    acc, sums = visit([(i, True, None), (jnp.maximum(i - 1, 0), False, i >= 1)],
                      jnp.zeros((tb, 2 * dh), F32), [zeros, zeros])

    def cond(carry):
        jb, _, sum0, sum1 = carry
        alive = jnp.minimum(jnp.min(sum0), jnp.min(sum1)) <= -EXP_ZERO_BELOW
        return jnp.logical_and(jb >= 0, alive)

    def body(carry):
        jb, acc, sum0, sum1 = carry
        acc, (sum0, sum1) = visit([(jb, False, None)], acc, [sum0, sum1])
        return jb - 1, acc, sum0, sum1

    _, acc, _, _ = lax.while_loop(cond, body, (i - 2, acc, sums[0], sums[1]))
    o_ref[...] = acc.astype(o_ref.dtype)


def sb_attention(qkv, *, batch, seq):
    n = qkv.shape[1]
    npair = 8
    qpb = seq // SB_T
    return pl.pallas_call(
        _sb_kernel,
        out_shape=jax.ShapeDtypeStruct((n, npair * LANES), BF16),
        grid=(batch, npair, qpb),
        in_specs=[
            pl.BlockSpec((None, SB_T, LANES), lambda b, p, i: (p, b * qpb + i, 0)),
            pl.BlockSpec((None, seq, LANES), lambda b, p, i: (npair + p, b, 0)),
            pl.BlockSpec((None, seq, LANES), lambda b, p, i: (2 * npair + p, b, 0)),
        ],
        out_specs=pl.BlockSpec((SB_T, LANES), lambda b, p, i: (b * qpb + i, p)),
        compiler_params=_cparams(("parallel", "parallel", "arbitrary")),
        name="sb_attention",
    )(qkv, qkv, qkv)


def kernel(x, gdn_w_in, gdn_conv_w, gdn_a_log, gdn_dt_bias, gdn_norm_w, gdn_w_out, sb_w_qkv, sb_w_o,
           mix_norm_w, ffn_norm_w, ffn_w_gate, ffn_w_up, ffn_w_down, final_norm_w):
    batch, seq, d = x.shape
    n = batch * seq
    depth = mix_norm_w.shape[0]
    n_proj = 6 * d
    xf = x.reshape(n, d)
    for i in range(depth):
        j = i // 2
        if i % 2 == 0:
            w_in = gdn_w_in[j]
            w_main = w_in[:, :n_proj].astype(BF16)
            w_ba = jnp.pad(w_in[:, n_proj:], ((0, 0), (0, LANES - (w_in.shape[1] - n_proj)))).astype(BF16)
            proj, ba = norm_matmul(xf, mix_norm_w[i], w_main, w_ba, tm=512, tn=2 * MXU_N, out_dtype=BF16,
                                   tile_major=False)
            og = gdn_core(proj, ba, gdn_conv_w[j].T, gdn_a_log[j], gdn_dt_bias[j], gdn_norm_w[j],
                          batch=batch, seq=seq)
            xf = matmul_residual(og, gdn_w_out[j].astype(BF16), xf, tm=512)
        else:
            qkv = norm_matmul(xf, mix_norm_w[i], sb_w_qkv[j].astype(BF16), tm=512, tn=2 * MXU_N,
                              out_dtype=BF16, tile_major=True)
            o = sb_attention(qkv, batch=batch, seq=seq)
            xf = matmul_residual(o, sb_w_o[j].astype(BF16), xf, tm=512)
        xf = ffn_residual(xf, ffn_norm_w[i], ffn_w_gate[i].astype(BF16), ffn_w_up[i].astype(BF16),
                          ffn_w_down[i].astype(BF16), final_norm_w, tm=512, tf=256,
                          final_norm=(i == depth - 1))
    return xf.reshape(batch, seq, d)
```

```python
import functools

import jax
import jax.numpy as jnp
from jax import lax
from jax.experimental import pallas as pl
from jax.experimental.pallas import tpu as pltpu

F32 = jnp.float32
BF16 = jnp.bfloat16

EPS = 1e-6
GDN_HEAD_DIM = 128
SB_HEAD_DIM = 64
CONV_WIDTH = 4
LANES = 128
VMEM_LIMIT_BYTES = 48 * 1024 * 1024
EXP_ZERO_BELOW = -104.0


def _cparams(sem, flags=None):
    return pltpu.CompilerParams(dimension_semantics=sem, vmem_limit_bytes=VMEM_LIMIT_BYTES, flags=flags)


def _silu(x):
    return x * (1.0 / (1.0 + jnp.exp(-x)))


def _softplus(x):
    return jnp.maximum(x, 0.0) + jnp.log(1.0 + jnp.exp(-jnp.abs(x)))


def _dot(a, b):
    return jnp.dot(a, b, preferred_element_type=F32)


def _dot_nt(a, b):
    return lax.dot_general(a, b, (((1,), (1,)), ((), ())), preferred_element_type=F32)


def _dot_tn(a, b):
    return lax.dot_general(a, b, (((0,), (0,)), ((), ())), preferred_element_type=F32)


def _split_bf16(x, terms):
    parts = []
    r = x
    for _ in range(terms):
        p = r.astype(BF16)
        parts.append(p)
        r = r - p.astype(F32)
    return parts


MXU_N = 256


def _resident(block_shape):
    zeros = (0,) * len(block_shape)
    return pl.BlockSpec(block_shape, lambda i: zeros, pipeline_mode=pl.Buffered(1))


def _norm_matmul_kernel(x_ref, nw_ref, w_ref, *rest, tn, tile_major):
    if len(rest) == 3:
        w2_ref, o_ref, o2_ref = rest
    else:
        (o_ref,) = rest
        w2_ref = o2_ref = None
    x = x_ref[...]
    y = x * lax.rsqrt(jnp.mean(x * x, axis=-1, keepdims=True) + EPS)
    h = (y * nw_ref[...]).astype(BF16)
    n_out = w_ref.shape[1]
    for c0 in range(0, n_out, tn):
        r = _dot(h, w_ref[:, c0:c0 + tn]).astype(o_ref.dtype)
        if tile_major:
            for l0 in range(0, tn, LANES):
                o_ref[(c0 + l0) // LANES] = r[:, l0:l0 + LANES]
        else:
            o_ref[:, c0:c0 + tn] = r
    if w2_ref is not None:
        o2_ref[...] = _dot(h, w2_ref[...])


def norm_matmul(x, norm_w, w, w2=None, *, tm, tn, out_dtype, tile_major):
    n, d = x.shape
    n_out = w.shape[1]
    if tile_major:
        out_shape = [jax.ShapeDtypeStruct((n_out // LANES, n, LANES), out_dtype)]
        out_specs = [pl.BlockSpec((n_out // LANES, tm, LANES), lambda i: (0, i, 0))]
    else:
        out_shape = [jax.ShapeDtypeStruct((n, n_out), out_dtype)]
        out_specs = [pl.BlockSpec((tm, n_out), lambda i: (i, 0))]
    operands = [x, norm_w.reshape(1, d), w]
    in_specs = [pl.BlockSpec((tm, d), lambda i: (i, 0)), _resident((1, d)), _resident((d, n_out))]
    if w2 is not None:
        n2 = w2.shape[1]
        operands.append(w2)
        in_specs.append(_resident((d, n2)))
        out_shape.append(jax.ShapeDtypeStruct((n, n2), F32))
        out_specs.append(pl.BlockSpec((tm, n2), lambda i: (i, 0)))
    out = pl.pallas_call(
        functools.partial(_norm_matmul_kernel, tn=tn, tile_major=tile_major),
        out_shape=out_shape,
        grid=(n // tm,),
        in_specs=in_specs,
        out_specs=out_specs,
        compiler_params=_cparams(("parallel",)),
        name="norm_matmul",
    )(*operands)
    return out if w2 is not None else out[0]


def _matmul_residual_kernel(a_ref, w_ref, x_ref, o_ref):
    o_ref[...] = x_ref[...] + _dot(a_ref[...], w_ref[...])


def matmul_residual(a, w, x, *, tm):
    n, k = a.shape
    d = w.shape[1]
    return pl.pallas_call(
        _matmul_residual_kernel,
        out_shape=jax.ShapeDtypeStruct((n, d), F32),
        grid=(n // tm,),
        in_specs=[
            pl.BlockSpec((tm, k), lambda i: (i, 0)),
            pl.BlockSpec((k, d), lambda i: (0, 0)),
            pl.BlockSpec((tm, d), lambda i: (i, 0)),
        ],
        out_specs=pl.BlockSpec((tm, d), lambda i: (i, 0)),
        compiler_params=_cparams(("parallel",)),
        name="matmul_residual",
    )(a, w, x)


def _ffn_kernel(x_ref, nw_ref, wg_ref, wu_ref, wd_ref, fw_ref, o_ref, *, tf, final_norm):
    x = x_ref[...]
    y = x * lax.rsqrt(jnp.mean(x * x, axis=-1, keepdims=True) + EPS)
    h = (y * nw_ref[...]).astype(BF16)
    f = wg_ref.shape[1]
    acc = x
    for c0 in range(0, f, tf):
        g = _dot(h, wg_ref[:, c0:c0 + tf])
        u = _dot(h, wu_ref[:, c0:c0 + tf])
        a = (_silu(g) * u).astype(BF16)
        acc = acc + _dot(a, wd_ref[c0:c0 + tf, :])
    if final_norm:
        acc = acc * lax.rsqrt(jnp.mean(acc * acc, axis=-1, keepdims=True) + EPS) * fw_ref[...]
    o_ref[...] = acc


def ffn_residual(x, norm_w, wg, wu, wd, final_w, *, tm, tf, final_norm):
    n, d = x.shape
    f = wg.shape[1]
    return pl.pallas_call(
        functools.partial(_ffn_kernel, tf=tf, final_norm=final_norm),
        out_shape=jax.ShapeDtypeStruct((n, d), F32),
        grid=(n // tm,),
        in_specs=[
            pl.BlockSpec((tm, d), lambda i: (i, 0)),
            _resident((1, d)),
            _resident((d, f)),
            _resident((d, f)),
            _resident((f, d)),
            _resident((1, d)),
        ],
        out_specs=pl.BlockSpec((tm, d), lambda i: (i, 0)),
        compiler_params=_cparams(("parallel",)),
        name="ffn_residual",
    )(x, norm_w.reshape(1, d), wg, wu, wd, final_w.reshape(1, d))


GDN_TB = 256
GDN_LEVELS = GDN_TB.bit_length() - 1
GDN_HB = 4
GDN_QK_HEADS = 8
HIST = 8
BETA_LANE = 0
A_LANE = 16


def _gdn_kernel(alog_ref, dtb_ref, q_ref, k_ref, v_ref, z_ref, ba_ref, cwq_ref, cwk_ref, cwv_ref,
                nw_ref, o_ref, xbuf, s_ref):
    hq = pl.program_id(1)
    t = pl.program_id(2)
    tb = GDN_TB
    dh = GDN_HEAD_DIM
    hb = GDN_HB
    k_off = hb * dh
    v_off = 2 * hb * dh

    @pl.when(t == 0)
    def _():
        xbuf[0:HIST, :] = jnp.zeros((HIST, 4 * hb * dh), F32)
        s_ref[...] = jnp.zeros_like(s_ref)

    xbuf[HIST:HIST + tb, 0:k_off] = q_ref[...].astype(F32)
    xbuf[HIST:HIST + tb, k_off:v_off] = k_ref[...].astype(F32)
    xbuf[HIST:HIST + tb, v_off:] = v_ref[...].astype(F32)

    def conv(lo, width, cw_ref, cw_lo):
        acc = None
        for j in range(CONV_WIDTH):
            start = HIST - (CONV_WIDTH - 1) + j
            term = xbuf[start:start + tb, lo:lo + width] * cw_ref[j:j + 1, cw_lo:cw_lo + width]
            acc = term if acc is None else acc + term
        return _silu(acc)

    lane = lax.broadcasted_iota(jnp.int32, (1, LANES), 1)
    first = 2 * hb * hq
    ba = pltpu.roll(ba_ref[...], (LANES - first) % LANES, 1)
    beta = 1.0 / (1.0 + jnp.exp(-ba))
    alog_row = jnp.zeros((1, LANES), F32)
    dtb_row = jnp.zeros((1, LANES), F32)
    for i in range(2 * hb):
        alog_row = jnp.where(lane == A_LANE + i, alog_ref[first + i], alog_row)
        dtb_row = jnp.where(lane == A_LANE + i, dtb_ref[first + i], dtb_row)
    g = -jnp.exp(alog_row) * _softplus(ba + dtb_row)

    ii = lax.broadcasted_iota(jnp.int32, (tb, tb), 0)
    jj = lax.broadcasted_iota(jnp.int32, (tb, tb), 1)
    lower = ii >= jj
    strict = ii > jj
    diag = ii == jj
    ij_xor = jnp.bitwise_xor(ii, jj)
    tri = jnp.where(lower, 1.0, 0.0).astype(BF16)
    gc = sum(_dot(tri, p) for p in _split_bf16(g, 3))
    gc_t = gc.T
    nw = nw_ref[...]

    heads = range(2 * hb)
    q16, k16, kf, kk, qk = [], [], [], [], []
    for hl in range(hb):
        qc = conv(hl * dh, dh, cwq_ref, hl * dh)
        kc = conv(k_off + hl * dh, dh, cwk_ref, hl * dh)
        q = qc * lax.rsqrt(jnp.sum(qc * qc, axis=-1, keepdims=True) + EPS) * (dh ** -0.5)
        k = kc * lax.rsqrt(jnp.sum(kc * kc, axis=-1, keepdims=True) + EPS)
        kf.append(k)
        q16.append(q.astype(BF16))
        k16.append(k.astype(BF16))
        kk.append(jnp.where(strict, _dot_nt(k16[hl], k16[hl]), 0.0))
        qk.append(_dot_nt(q16[hl], k16[hl]))

    gcol = [gc[:, A_LANE + hv:A_LANE + hv + 1] for hv in heads]
    grow = [gc_t[A_LANE + hv:A_LANE + hv + 1, :] for hv in heads]
    bcol = [beta[:, BETA_LANE + hv:BETA_LANE + hv + 1] for hv in heads]
    glast = [grow[hv][:, tb - 1:tb] for hv in heads]
    decay = [jnp.where(lower, jnp.exp(gcol[hv] - grow[hv]), 0.0) for hv in heads]

    lmat = [(bcol[hv] * kk[hv // 2]) * decay[hv] for hv in heads]
    l16 = [lmat[hv].astype(BF16) for hv in heads]
    top_bit = (jnp.int32(31) - lax.clz(ij_xor))
    eye = jnp.where(diag, 1.0, 0.0)
    dinv = [eye - jnp.where(top_bit == 0, lmat[hv], 0.0) for hv in heads]
    for level in range(1, GDN_LEVELS):
        in_c = top_bit == level
        d16 = [dinv[hv].astype(BF16) for hv in heads]
        ld = [_dot(l16[hv], d16[hv]).astype(BF16) for hv in heads]
        dinv = [dinv[hv] - jnp.where(in_c, _dot(d16[hv], ld[hv]), 0.0) for hv in heads]
    r16 = [(dinv[hv] - eye).astype(BF16) for hv in heads]

    vb = [bcol[hv] * conv(v_off + hv * dh, dh, cwv_ref, hv * dh) for hv in heads]
    kb = [(bcol[hv] * jnp.exp(gcol[hv])) * kf[hv // 2] for hv in heads]
    ruw = [_dot(r16[hv], jnp.concatenate([vb[hv].astype(BF16), kb[hv].astype(BF16)], axis=1))
           for hv in heads]
    u = [vb[hv] + ruw[hv][:, :dh] for hv in heads]
    w = [kb[hv] + ruw[hv][:, dh:] for hv in heads]
    attn = [(qk[hv // 2] * decay[hv]).astype(BF16) for hv in heads]

    s = [s_ref[hv] for hv in heads]
    s16 = [s[hv].astype(BF16) for hv in heads]
    v_new = [u[hv] - _dot(w[hv].astype(BF16), s16[hv]) for hv in heads]
    o = [jnp.exp(gcol[hv]) * _dot(q16[hv // 2], s16[hv]) + _dot(attn[hv], v_new[hv].astype(BF16))
         for hv in heads]
    for hv in heads:
        k_dec = (jnp.exp(glast[hv] - gcol[hv]) * v_new[hv]).astype(BF16)
        s_ref[hv] = s[hv] * jnp.exp(glast[hv]) + _dot_tn(k16[hv // 2], k_dec)
    for hv in heads:
        zs = z_ref[:, hv * dh:(hv + 1) * dh].astype(F32)
        on = o[hv] * lax.rsqrt(jnp.mean(o[hv] * o[hv], axis=-1, keepdims=True) + EPS) * nw
        o_ref[:, hv * dh:(hv + 1) * dh] = (on * _silu(zs)).astype(o_ref.dtype)

    xbuf[0:HIST, :] = xbuf[tb:tb + HIST, :]


def gdn_core(proj, ba, conv_w_t, a_log, dt_bias, norm_w, *, batch, seq):
    n = proj.shape[0]
    dh = GDN_HEAD_DIM
    hb = GDN_HB
    ng = GDN_QK_HEADS // hb
    tpb = seq // GDN_TB
    row = lambda b, h, t: b * tpb + t
    return pl.pallas_call(
        _gdn_kernel,
        out_shape=jax.ShapeDtypeStruct((n, 2 * GDN_QK_HEADS * dh), BF16),
        grid=(batch, ng, tpb),
        in_specs=[
            pl.BlockSpec(memory_space=pltpu.SMEM),
            pl.BlockSpec(memory_space=pltpu.SMEM),
            pl.BlockSpec((GDN_TB, hb * dh), lambda b, h, t: (row(b, h, t), h)),
            pl.BlockSpec((GDN_TB, hb * dh), lambda b, h, t: (row(b, h, t), ng + h)),
            pl.BlockSpec((GDN_TB, 2 * hb * dh), lambda b, h, t: (row(b, h, t), ng + h)),
            pl.BlockSpec((GDN_TB, 2 * hb * dh), lambda b, h, t: (row(b, h, t), 2 * ng + h)),
            pl.BlockSpec((GDN_TB, LANES), lambda b, h, t: (row(b, h, t), 0)),
            pl.BlockSpec((CONV_WIDTH, hb * dh), lambda b, h, t: (0, h)),
            pl.BlockSpec((CONV_WIDTH, hb * dh), lambda b, h, t: (0, ng + h)),
            pl.BlockSpec((CONV_WIDTH, 2 * hb * dh), lambda b, h, t: (0, ng + h)),
            pl.BlockSpec((1, dh), lambda b, h, t: (0, 0)),
        ],
        out_specs=pl.BlockSpec((GDN_TB, 2 * hb * dh), lambda b, h, t: (row(b, h, t), h)),
        scratch_shapes=[
            pltpu.VMEM((GDN_TB + HIST, 4 * hb * dh), F32),
            pltpu.VMEM((2 * hb, dh, dh), F32),
        ],
        compiler_params=_cparams(("parallel", "parallel", "arbitrary")),
        name="gdn_core",
    )(a_log, dt_bias, proj, proj, proj, proj, ba, conv_w_t, conv_w_t, conv_w_t,
      norm_w.reshape(1, dh))


SB_T = 256


def _sb_kernel(q_ref, k_ref, v_ref, o_ref):
    i = pl.program_id(2)
    tb, dh = SB_T, SB_HEAD_DIM
    lane = lax.broadcasted_iota(jnp.int32, (1, 2 * dh), 1)
    head_mask = [lane < dh, lane >= dh]
    zero16 = jnp.zeros((), BF16)
    q = q_ref[...] * jnp.asarray(dh ** -0.5, BF16)
    qh = [jnp.where(m, q, zero16) for m in head_mask]

    ri = lax.broadcasted_iota(jnp.int32, (tb, tb), 0)
    ci = lax.broadcasted_iota(jnp.int32, (tb, tb), 1)
    upper = jnp.where(ri > ci, 1.0, 0.0).astype(BF16)
    upper2 = jnp.concatenate([upper, upper], axis=0)
    causal = ci < ri

    def visit(blocks, acc, sums):
        pairs = [(b, h) for b in range(len(blocks)) for h in range(2)]
        kbs, vbs = [], []
        for jb, _, _ in blocks:
            s0 = pl.multiple_of(jb * tb, tb)
            kbs.append(k_ref[pl.ds(s0, tb), :])
            vbs.append(v_ref[pl.ds(s0, tb), :])
        z2 = {p: _dot_nt(qh[p[1]], kbs[p[0]]) for p in pairs}
        sp2 = {p: _softplus(z2[p]) for p in pairs}
        for p in pairs:
            _, diagonal, exists = blocks[p[0]]
            if diagonal:
                sp2[p] = jnp.where(causal, sp2[p], 0.0)
            if exists is not None:
                sp2[p] = jnp.where(exists, sp2[p], 0.0)
        within = {}
        for p in pairs:
            hi, lo = _split_bf16(sp2[p], 2)
            within[p] = _dot(jnp.concatenate([hi, lo], axis=1), upper2)
        total = {p: jnp.sum(sp2[p], axis=-1, keepdims=True) for p in pairs}
        sums = list(sums)
        att = {}
        for p in pairs:
            att[p] = jnp.exp(z2[p] - sp2[p] - within[p] - sums[p[1]])
            sums[p[1]] = sums[p[1]] + total[p]
            _, diagonal, exists = blocks[p[0]]
            if diagonal:
                att[p] = jnp.where(causal, att[p], 0.0)
            if exists is not None:
                att[p] = jnp.where(exists, att[p], 0.0)
        for p in pairs:
            acc = acc + _dot(att[p].astype(BF16), jnp.where(head_mask[p[1]], vbs[p[0]], zero16))
        return acc, sums

    zeros = jnp.zeros((tb, 1), F32)
    acc, sums = visit([(i, True, None), (jnp.maximum(i - 1, 0), False, i >= 1)],
                      jnp.zeros((tb, 2 * dh), F32), [zeros, zeros])

    def cond(carry):
        jb, _, sum0, sum1 = carry
        alive = jnp.minimum(jnp.min(sum0), jnp.min(sum1)) <= -EXP_ZERO_BELOW
        return jnp.logical_and(jb >= 0, alive)

    def body(carry):
        jb, acc, sum0, sum1 = carry
        acc, (sum0, sum1) = visit([(jb, False, None)], acc, [sum0, sum1])
        return jb - 1, acc, sum0, sum1

    _, acc, _, _ = lax.while_loop(cond, body, (i - 2, acc, sums[0], sums[1]))
    o_ref[...] = acc.astype(o_ref.dtype)


def sb_attention(qkv, *, batch, seq):
    n = qkv.shape[1]
    npair = 8
    qpb = seq // SB_T
    return pl.pallas_call(
        _sb_kernel,
        out_shape=jax.ShapeDtypeStruct((n, npair * LANES), BF16),
        grid=(batch, npair, qpb),
        in_specs=[
            pl.BlockSpec((None, SB_T, LANES), lambda b, p, i: (p, b * qpb + i, 0)),
            pl.BlockSpec((None, seq, LANES), lambda b, p, i: (npair + p, b, 0)),
            pl.BlockSpec((None, seq, LANES), lambda b, p, i: (2 * npair + p, b, 0)),
        ],
        out_specs=pl.BlockSpec((SB_T, LANES), lambda b, p, i: (b * qpb + i, p)),
        compiler_params=_cparams(("parallel", "parallel", "arbitrary")),
        name="sb_attention",
    )(qkv, qkv, qkv)


def kernel(x, gdn_w_in, gdn_conv_w, gdn_a_log, gdn_dt_bias, gdn_norm_w, gdn_w_out, sb_w_qkv, sb_w_o,
           mix_norm_w, ffn_norm_w, ffn_w_gate, ffn_w_up, ffn_w_down, final_norm_w):
    batch, seq, d = x.shape
    n = batch * seq
    depth = mix_norm_w.shape[0]
    n_proj = 6 * d
    xf = x.reshape(n, d)
    for i in range(depth):
        j = i // 2
        if i % 2 == 0:
            w_in = gdn_w_in[j]
            w_main = w_in[:, :n_proj].astype(BF16)
            w_ba = jnp.pad(w_in[:, n_proj:], ((0, 0), (0, LANES - (w_in.shape[1] - n_proj)))).astype(BF16)
            proj, ba = norm_matmul(xf, mix_norm_w[i], w_main, w_ba, tm=512, tn=2 * MXU_N, out_dtype=BF16,
                                   tile_major=False)
            og = gdn_core(proj, ba, gdn_conv_w[j].T, gdn_a_log[j], gdn_dt_bias[j], gdn_norm_w[j],
                          batch=batch, seq=seq)
            xf = matmul_residual(og, gdn_w_out[j].astype(BF16), xf, tm=512)
        else:
            qkv = norm_matmul(xf, mix_norm_w[i], sb_w_qkv[j].astype(BF16), tm=512, tn=2 * MXU_N,
                              out_dtype=BF16, tile_major=True)
            o = sb_attention(qkv, batch=batch, seq=seq)
            xf = matmul_residual(o, sb_w_o[j].astype(BF16), xf, tm=512)
        xf = ffn_residual(xf, ffn_norm_w[i], ffn_w_gate[i].astype(BF16), ffn_w_up[i].astype(BF16),
                          ffn_w_down[i].astype(BF16), final_norm_w, tm=512, tf=256,
                          final_norm=(i == depth - 1))
    return xf.reshape(batch, seq, d)
```

```python
import functools

import jax
import jax.numpy as jnp
from jax import lax
from jax.experimental import pallas as pl
from jax.experimental.pallas import tpu as pltpu

F32 = jnp.float32
BF16 = jnp.bfloat16

EPS = 1e-6
GDN_HEAD_DIM = 128
SB_HEAD_DIM = 64
CONV_WIDTH = 4
LANES = 128
VMEM_LIMIT_BYTES = 48 * 1024 * 1024
EXP_ZERO_BELOW = -104.0


def _cparams(sem, flags=None):
    return pltpu.CompilerParams(dimension_semantics=sem, vmem_limit_bytes=VMEM_LIMIT_BYTES, flags=flags)


def _silu(x):
    return x * (1.0 / (1.0 + jnp.exp(-x)))


def _softplus(x):
    return jnp.maximum(x, 0.0) + jnp.log(1.0 + jnp.exp(-jnp.abs(x)))


def _dot(a, b):
    return jnp.dot(a, b, preferred_element_type=F32)


def _dot_nt(a, b):
    return lax.dot_general(a, b, (((1,), (1,)), ((), ())), preferred_element_type=F32)


def _dot_tn(a, b):
    return lax.dot_general(a, b, (((0,), (0,)), ((), ())), preferred_element_type=F32)


def _split_bf16(x, terms):
    parts = []
    r = x
    for _ in range(terms):
        p = r.astype(BF16)
        parts.append(p)
        r = r - p.astype(F32)
    return parts


MXU_N = 256


def _resident(block_shape):
    zeros = (0,) * len(block_shape)
    return pl.BlockSpec(block_shape, lambda i: zeros, pipeline_mode=pl.Buffered(1))


def _norm_matmul_kernel(x_ref, nw_ref, w_ref, *rest, tn, tile_major):
    if len(rest) == 3:
        w2_ref, o_ref, o2_ref = rest
    else:
        (o_ref,) = rest
        w2_ref = o2_ref = None
    x = x_ref[...]
    y = x * lax.rsqrt(jnp.mean(x * x, axis=-1, keepdims=True) + EPS)
    h = (y * nw_ref[...]).astype(BF16)
    n_out = w_ref.shape[1]
    for c0 in range(0, n_out, tn):
        r = _dot(h, w_ref[:, c0:c0 + tn]).astype(o_ref.dtype)
        if tile_major:
            for l0 in range(0, tn, LANES):
                o_ref[(c0 + l0) // LANES] = r[:, l0:l0 + LANES]
        else:
            o_ref[:, c0:c0 + tn] = r
    if w2_ref is not None:
        o2_ref[...] = _dot(h, w2_ref[...])


def norm_matmul(x, norm_w, w, w2=None, *, tm, tn, out_dtype, tile_major):
    n, d = x.shape
    n_out = w.shape[1]
    if tile_major:
        out_shape = [jax.ShapeDtypeStruct((n_out // LANES, n, LANES), out_dtype)]
        out_specs = [pl.BlockSpec((n_out // LANES, tm, LANES), lambda i: (0, i, 0))]
    else:
        out_shape = [jax.ShapeDtypeStruct((n, n_out), out_dtype)]
        out_specs = [pl.BlockSpec((tm, n_out), lambda i: (i, 0))]
    operands = [x, norm_w.reshape(1, d), w]
    in_specs = [pl.BlockSpec((tm, d), lambda i: (i, 0)), _resident((1, d)), _resident((d, n_out))]
    if w2 is not None:
        n2 = w2.shape[1]
        operands.append(w2)
        in_specs.append(_resident((d, n2)))
        out_shape.append(jax.ShapeDtypeStruct((n, n2), F32))
        out_specs.append(pl.BlockSpec((tm, n2), lambda i: (i, 0)))
    out = pl.pallas_call(
        functools.partial(_norm_matmul_kernel, tn=tn, tile_major=tile_major),
        out_shape=out_shape,
        grid=(n // tm,),
        in_specs=in_specs,
        out_specs=out_specs,
        compiler_params=_cparams(("parallel",)),
        name="norm_matmul",
    )(*operands)
    return out if w2 is not None else out[0]


def _ffn_kernel(x_ref, a_ref, wo_ref, nw_ref, wg_ref, wu_ref, wd_ref, fw_ref, o_ref, *, tf, final_norm):
    x = x_ref[...] + _dot(a_ref[...], wo_ref[...])
    y = x * lax.rsqrt(jnp.mean(x * x, axis=-1, keepdims=True) + EPS)
    h = (y * nw_ref[...]).astype(BF16)
    f = wg_ref.shape[1]
    acc = x
    for c0 in range(0, f, tf):
        g = _dot(h, wg_ref[:, c0:c0 + tf])
        u = _dot(h, wu_ref[:, c0:c0 + tf])
        a = (_silu(g) * u).astype(BF16)
        acc = acc + _dot(a, wd_ref[c0:c0 + tf, :])
    if final_norm:
        acc = acc * lax.rsqrt(jnp.mean(acc * acc, axis=-1, keepdims=True) + EPS) * fw_ref[...]
    o_ref[...] = acc


def mixer_out_ffn(x, a, wo, norm_w, wg, wu, wd, final_w, *, tm, tf, final_norm):
    n, d = x.shape
    f = wg.shape[1]
    ka = a.shape[1]
    return pl.pallas_call(
        functools.partial(_ffn_kernel, tf=tf, final_norm=final_norm),
        out_shape=jax.ShapeDtypeStruct((n, d), F32),
        grid=(n // tm,),
        in_specs=[
            pl.BlockSpec((tm, d), lambda i: (i, 0)),
            pl.BlockSpec((tm, ka), lambda i: (i, 0)),
            _resident((ka, d)),
            _resident((1, d)),
            _resident((d, f)),
            _resident((d, f)),
            _resident((f, d)),
            _resident((1, d)),
        ],
        out_specs=pl.BlockSpec((tm, d), lambda i: (i, 0)),
        compiler_params=_cparams(("parallel",)),
        name="mixer_out_ffn",
    )(x, a, wo, norm_w.reshape(1, d), wg, wu, wd, final_w.reshape(1, d))


GDN_TB = 256
GDN_LEVELS = GDN_TB.bit_length() - 1
GDN_HB = 4
GDN_QK_HEADS = 8
HIST = 8
BETA_LANE = 0
A_LANE = 16


def _gdn_kernel(alog_ref, dtb_ref, q_ref, k_ref, v_ref, z_ref, ba_ref, cwq_ref, cwk_ref, cwv_ref,
                nw_ref, o_ref, xbuf, s_ref):
    hq = pl.program_id(1)
    t = pl.program_id(2)
    tb = GDN_TB
    dh = GDN_HEAD_DIM
    hb = GDN_HB
    k_off = hb * dh
    v_off = 2 * hb * dh

    @pl.when(t == 0)
    def _():
        xbuf[0:HIST, :] = jnp.zeros((HIST, 4 * hb * dh), F32)
        s_ref[...] = jnp.zeros_like(s_ref)

    xbuf[HIST:HIST + tb, 0:k_off] = q_ref[...].astype(F32)
    xbuf[HIST:HIST + tb, k_off:v_off] = k_ref[...].astype(F32)
    xbuf[HIST:HIST + tb, v_off:] = v_ref[...].astype(F32)

    def conv(lo, width, cw_ref, cw_lo):
        acc = None
        for j in range(CONV_WIDTH):
            start = HIST - (CONV_WIDTH - 1) + j
            term = xbuf[start:start + tb, lo:lo + width] * cw_ref[j:j + 1, cw_lo:cw_lo + width]
            acc = term if acc is None else acc + term
        return _silu(acc)

    lane = lax.broadcasted_iota(jnp.int32, (1, LANES), 1)
    first = 2 * hb * hq
    ba = pltpu.roll(ba_ref[...], (LANES - first) % LANES, 1)
    beta = 1.0 / (1.0 + jnp.exp(-ba))
    alog_row = jnp.zeros((1, LANES), F32)
    dtb_row = jnp.zeros((1, LANES), F32)
    for i in range(2 * hb):
        alog_row = jnp.where(lane == A_LANE + i, alog_ref[first + i], alog_row)
        dtb_row = jnp.where(lane == A_LANE + i, dtb_ref[first + i], dtb_row)
    g = -jnp.exp(alog_row) * _softplus(ba + dtb_row)

    ii = lax.broadcasted_iota(jnp.int32, (tb, tb), 0)
    jj = lax.broadcasted_iota(jnp.int32, (tb, tb), 1)
    lower = ii >= jj
    strict = ii > jj
    diag = ii == jj
    ij_xor = jnp.bitwise_xor(ii, jj)
    tri = jnp.where(lower, 1.0, 0.0).astype(BF16)
    gc = sum(_dot(tri, p) for p in _split_bf16(g, 3))
    gc_t = gc.T
    nw = nw_ref[...]

    heads = range(2 * hb)
    q16, k16, kf, kk, qk = [], [], [], [], []
    for hl in range(hb):
        qc = conv(hl * dh, dh, cwq_ref, hl * dh)
        kc = conv(k_off + hl * dh, dh, cwk_ref, hl * dh)
        q = qc * lax.rsqrt(jnp.sum(qc * qc, axis=-1, keepdims=True) + EPS) * (dh ** -0.5)
        k = kc * lax.rsqrt(jnp.sum(kc * kc, axis=-1, keepdims=True) + EPS)
        kf.append(k)
        q16.append(q.astype(BF16))
        k16.append(k.astype(BF16))
        kk.append(jnp.where(strict, _dot_nt(k16[hl], k16[hl]), 0.0))
        qk.append(_dot_nt(q16[hl], k16[hl]))

    gcol = [gc[:, A_LANE + hv:A_LANE + hv + 1] for hv in heads]
    grow = [gc_t[A_LANE + hv:A_LANE + hv + 1, :] for hv in heads]
    bcol = [beta[:, BETA_LANE + hv:BETA_LANE + hv + 1] for hv in heads]
    glast = [grow[hv][:, tb - 1:tb] for hv in heads]
    decay = [jnp.where(lower, jnp.exp(gcol[hv] - grow[hv]), 0.0) for hv in heads]

    lmat = [(bcol[hv] * kk[hv // 2]) * decay[hv] for hv in heads]
    l16 = [lmat[hv].astype(BF16) for hv in heads]
    top_bit = (jnp.int32(31) - lax.clz(ij_xor))
    eye = jnp.where(diag, 1.0, 0.0)
    dinv = [eye - jnp.where(top_bit == 0, lmat[hv], 0.0) for hv in heads]
    for level in range(1, GDN_LEVELS):
        in_c = top_bit == level
        d16 = [dinv[hv].astype(BF16) for hv in heads]
        ld = [_dot(l16[hv], d16[hv]).astype(BF16) for hv in heads]
        dinv = [dinv[hv] - jnp.where(in_c, _dot(d16[hv], ld[hv]), 0.0) for hv in heads]
    r16 = [(dinv[hv] - eye).astype(BF16) for hv in heads]

    vb = [bcol[hv] * conv(v_off + hv * dh, dh, cwv_ref, hv * dh) for hv in heads]
    kb = [(bcol[hv] * jnp.exp(gcol[hv])) * kf[hv // 2] for hv in heads]
    ruw = [_dot(r16[hv], jnp.concatenate([vb[hv].astype(BF16), kb[hv].astype(BF16)], axis=1))
           for hv in heads]
    u = [vb[hv] + ruw[hv][:, :dh] for hv in heads]
    w = [kb[hv] + ruw[hv][:, dh:] for hv in heads]
    attn = [(qk[hv // 2] * decay[hv]).astype(BF16) for hv in heads]

    s = [s_ref[hv] for hv in heads]
    s16 = [s[hv].astype(BF16) for hv in heads]
    v_new = [u[hv] - _dot(w[hv].astype(BF16), s16[hv]) for hv in heads]
    o = [jnp.exp(gcol[hv]) * _dot(q16[hv // 2], s16[hv]) + _dot(attn[hv], v_new[hv].astype(BF16))
         for hv in heads]
    for hv in heads:
        k_dec = (jnp.exp(glast[hv] - gcol[hv]) * v_new[hv]).astype(BF16)
        s_ref[hv] = s[hv] * jnp.exp(glast[hv]) + _dot_tn(k16[hv // 2], k_dec)
    for hv in heads:
        zs = z_ref[:, hv * dh:(hv + 1) * dh].astype(F32)
        on = o[hv] * lax.rsqrt(jnp.mean(o[hv] * o[hv], axis=-1, keepdims=True) + EPS) * nw
        o_ref[:, hv * dh:(hv + 1) * dh] = (on * _silu(zs)).astype(o_ref.dtype)

    xbuf[0:HIST, :] = xbuf[tb:tb + HIST, :]


def gdn_core(proj, ba, conv_w_t, a_log, dt_bias, norm_w, *, batch, seq):
    n = proj.shape[0]
    dh = GDN_HEAD_DIM
    hb = GDN_HB
    ng = GDN_QK_HEADS // hb
    tpb = seq // GDN_TB
    row = lambda b, h, t: b * tpb + t
    return pl.pallas_call(
        _gdn_kernel,
        out_shape=jax.ShapeDtypeStruct((n, 2 * GDN_QK_HEADS * dh), BF16),
        grid=(batch, ng, tpb),
        in_specs=[
            pl.BlockSpec(memory_space=pltpu.SMEM),
            pl.BlockSpec(memory_space=pltpu.SMEM),
            pl.BlockSpec((GDN_TB, hb * dh), lambda b, h, t: (row(b, h, t), h)),
            pl.BlockSpec((GDN_TB, hb * dh), lambda b, h, t: (row(b, h, t), ng + h)),
            pl.BlockSpec((GDN_TB, 2 * hb * dh), lambda b, h, t: (row(b, h, t), ng + h)),
            pl.BlockSpec((GDN_TB, 2 * hb * dh), lambda b, h, t: (row(b, h, t), 2 * ng + h)),
            pl.BlockSpec((GDN_TB, LANES), lambda b, h, t: (row(b, h, t), 0)),
            pl.BlockSpec((CONV_WIDTH, hb * dh), lambda b, h, t: (0, h)),
            pl.BlockSpec((CONV_WIDTH, hb * dh), lambda b, h, t: (0, ng + h)),
            pl.BlockSpec((CONV_WIDTH, 2 * hb * dh), lambda b, h, t: (0, ng + h)),
            pl.BlockSpec((1, dh), lambda b, h, t: (0, 0)),
        ],
        out_specs=pl.BlockSpec((GDN_TB, 2 * hb * dh), lambda b, h, t: (row(b, h, t), h)),
        scratch_shapes=[
            pltpu.VMEM((GDN_TB + HIST, 4 * hb * dh), F32),
            pltpu.VMEM((2 * hb, dh, dh), F32),
        ],
        compiler_params=_cparams(("parallel", "parallel", "arbitrary")),
        name="gdn_core",
    )(a_log, dt_bias, proj, proj, proj, proj, ba, conv_w_t, conv_w_t, conv_w_t,
      norm_w.reshape(1, dh))


SB_T = 256
SB_NP = 2
SB_PAIRS = 8


def _sb_kernel(q_ref, k_ref, v_ref, o_ref):
    i = pl.program_id(2)
    tb, dh = SB_T, SB_HEAD_DIM
    lane = lax.broadcasted_iota(jnp.int32, (1, 2 * dh), 1)
    head_mask = [lane < dh, lane >= dh]
    zero16 = jnp.zeros((), BF16)
    scale = jnp.asarray(dh ** -0.5, BF16)
    heads = [(pr, h) for pr in range(SB_NP) for h in range(2)]
    qh = {(pr, h): jnp.where(head_mask[h], q_ref[pr] * scale, zero16) for pr, h in heads}

    ri = lax.broadcasted_iota(jnp.int32, (tb, tb), 0)
    ci = lax.broadcasted_iota(jnp.int32, (tb, tb), 1)
    upper = jnp.where(ri > ci, 1.0, 0.0).astype(BF16)
    upper2 = jnp.concatenate([upper, upper], axis=0)
    causal = ci < ri

    def visit(blocks, accs, sums):
        chains = [(b, hd) for b in range(len(blocks)) for hd in heads]
        kbs, vbs = {}, {}
        for b, (jb, _, _) in enumerate(blocks):
            s0 = pl.multiple_of(jb * tb, tb)
            for pr in range(SB_NP):
                kbs[b, pr] = k_ref[pr, pl.ds(s0, tb), :]
                vbs[b, pr] = v_ref[pr, pl.ds(s0, tb), :]
        z2 = {c: _dot_nt(qh[c[1]], kbs[c[0], c[1][0]]) for c in chains}
        sp2 = {c: _softplus(z2[c]) for c in chains}
        for c in chains:
            _, diagonal, exists = blocks[c[0]]
            if diagonal:
                sp2[c] = jnp.where(causal, sp2[c], 0.0)
            if exists is not None:
                sp2[c] = jnp.where(exists, sp2[c], 0.0)
        within = {}
        for c in chains:
            hi, lo = _split_bf16(sp2[c], 2)
            within[c] = _dot(jnp.concatenate([hi, lo], axis=1), upper2)
        total = {c: jnp.sum(sp2[c], axis=-1, keepdims=True) for c in chains}
        sums = dict(sums)
        att = {}
        for c in chains:
            att[c] = jnp.exp(z2[c] - sp2[c] - within[c] - sums[c[1]])
            sums[c[1]] = sums[c[1]] + total[c]
            _, diagonal, exists = blocks[c[0]]
            if diagonal:
                att[c] = jnp.where(causal, att[c], 0.0)
            if exists is not None:
                att[c] = jnp.where(exists, att[c], 0.0)
        accs = list(accs)
        for c in chains:
            pr, h = c[1]
            accs[pr] = accs[pr] + _dot(att[c].astype(BF16), jnp.where(head_mask[h], vbs[c[0], pr], zero16))
        return accs, sums

    zeros = jnp.zeros((tb, 1), F32)
    accs, sums = visit([(i, True, None), (jnp.maximum(i - 1, 0), False, i >= 1)],
                       [jnp.zeros((tb, 2 * dh), F32)] * SB_NP, {hd: zeros for hd in heads})

    def cond(carry):
        jb, _, sum_list = carry
        smallest = functools.reduce(jnp.minimum, [jnp.min(x) for x in sum_list])
        return jnp.logical_and(jb >= 0, smallest <= -EXP_ZERO_BELOW)

    def body(carry):
        jb, accs, sum_list = carry
        accs, sums = visit([(jb, False, None)], accs, dict(zip(heads, sum_list)))
        return jb - 1, accs, [sums[hd] for hd in heads]

    _, accs, _ = lax.while_loop(cond, body, (i - 2, accs, [sums[hd] for hd in heads]))
    for pr in range(SB_NP):
        o_ref[:, pr * LANES:(pr + 1) * LANES] = accs[pr].astype(o_ref.dtype)


def sb_attention(qkv, *, batch, seq):
    n = qkv.shape[1]
    ng = SB_PAIRS // SB_NP
    qpb = seq // SB_T
    return pl.pallas_call(
        _sb_kernel,
        out_shape=jax.ShapeDtypeStruct((n, SB_PAIRS * LANES), BF16),
        grid=(batch, ng, qpb),
        in_specs=[
            pl.BlockSpec((SB_NP, SB_T, LANES), lambda b, p, i: (p, b * qpb + i, 0)),
            pl.BlockSpec((SB_NP, seq, LANES), lambda b, p, i: (ng + p, b, 0)),
            pl.BlockSpec((SB_NP, seq, LANES), lambda b, p, i: (2 * ng + p, b, 0)),
        ],
        out_specs=pl.BlockSpec((SB_T, SB_NP * LANES), lambda b, p, i: (b * qpb + i, p)),
        compiler_params=_cparams(("parallel", "parallel", "arbitrary")),
        name="sb_attention",
    )(qkv, qkv, qkv)


def kernel(x, gdn_w_in, gdn_conv_w, gdn_a_log, gdn_dt_bias, gdn_norm_w, gdn_w_out, sb_w_qkv, sb_w_o,
           mix_norm_w, ffn_norm_w, ffn_w_gate, ffn_w_up, ffn_w_down, final_norm_w):
    batch, seq, d = x.shape
    n = batch * seq
    depth = mix_norm_w.shape[0]
    n_proj = 6 * d
    xf = x.reshape(n, d)
    for i in range(depth):
        j = i // 2
        if i % 2 == 0:
            w_in = gdn_w_in[j]
            w_main = w_in[:, :n_proj].astype(BF16)
            w_ba = jnp.pad(w_in[:, n_proj:], ((0, 0), (0, LANES - (w_in.shape[1] - n_proj)))).astype(BF16)
            proj, ba = norm_matmul(xf, mix_norm_w[i], w_main, w_ba, tm=512, tn=2 * MXU_N, out_dtype=BF16,
                                   tile_major=False)
            mixed = gdn_core(proj, ba, gdn_conv_w[j].T, gdn_a_log[j], gdn_dt_bias[j], gdn_norm_w[j],
                             batch=batch, seq=seq)
            w_out = gdn_w_out[j]
        else:
            qkv = norm_matmul(xf, mix_norm_w[i], sb_w_qkv[j].astype(BF16), tm=512, tn=2 * MXU_N,
                              out_dtype=BF16, tile_major=True)
            mixed = sb_attention(qkv, batch=batch, seq=seq)
            w_out = sb_w_o[j]
        xf = mixer_out_ffn(xf, mixed, w_out.astype(BF16), ffn_norm_w[i], ffn_w_gate[i].astype(BF16),
                           ffn_w_up[i].astype(BF16), ffn_w_down[i].astype(BF16), final_norm_w,
                           tm=512, tf=256, final_norm=(i == depth - 1))
    return xf.reshape(batch, seq, d)
```

```python
import functools

import jax
import jax.numpy as jnp
from jax import lax
from jax.experimental import pallas as pl
from jax.experimental.pallas import tpu as pltpu

F32 = jnp.float32
BF16 = jnp.bfloat16

EPS = 1e-6
GDN_HEAD_DIM = 128
SB_HEAD_DIM = 64
CONV_WIDTH = 4
LANES = 128
VMEM_LIMIT_BYTES = 48 * 1024 * 1024
EXP_ZERO_BELOW = -104.0


def _cparams(sem, flags=None):
    return pltpu.CompilerParams(dimension_semantics=sem, vmem_limit_bytes=VMEM_LIMIT_BYTES, flags=flags)


def _silu(x):
    return x * (1.0 / (1.0 + jnp.exp(-x)))


def _softplus(x):
    return jnp.maximum(x, 0.0) + jnp.log(1.0 + jnp.exp(-jnp.abs(x)))


def _dot(a, b):
    return jnp.dot(a, b, preferred_element_type=F32)


def _dot_nt(a, b):
    return lax.dot_general(a, b, (((1,), (1,)), ((), ())), preferred_element_type=F32)


def _dot_tn(a, b):
    return lax.dot_general(a, b, (((0,), (0,)), ((), ())), preferred_element_type=F32)


def _split_bf16(x, terms):
    parts = []
    r = x
    for _ in range(terms):
        p = r.astype(BF16)
        parts.append(p)
        r = r - p.astype(F32)
    return parts


MXU_N = 256


def _resident(block_shape):
    zeros = (0,) * len(block_shape)
    return pl.BlockSpec(block_shape, lambda i: zeros, pipeline_mode=pl.Buffered(1))


def _norm_matmul_kernel(x_ref, nw_ref, w_ref, *rest, tn, tile_major):
    if len(rest) == 3:
        w2_ref, o_ref, o2_ref = rest
    else:
        (o_ref,) = rest
        w2_ref = o2_ref = None
    x = x_ref[...]
    y = x * lax.rsqrt(jnp.mean(x * x, axis=-1, keepdims=True) + EPS)
    h = (y * nw_ref[...]).astype(BF16)
    n_out = w_ref.shape[1]
    for c0 in range(0, n_out, tn):
        r = _dot(h, w_ref[:, c0:c0 + tn]).astype(o_ref.dtype)
        if tile_major:
            for l0 in range(0, tn, LANES):
                o_ref[(c0 + l0) // LANES] = r[:, l0:l0 + LANES]
        else:
            o_ref[:, c0:c0 + tn] = r
    if w2_ref is not None:
        o2_ref[...] = _dot(h, w2_ref[...])


def norm_matmul(x, norm_w, w, w2=None, *, tm, tn, out_dtype, tile_major):
    n, d = x.shape
    n_out = w.shape[1]
    if tile_major:
        out_shape = [jax.ShapeDtypeStruct((n_out // LANES, n, LANES), out_dtype)]
        out_specs = [pl.BlockSpec((n_out // LANES, tm, LANES), lambda i: (0, i, 0))]
    else:
        out_shape = [jax.ShapeDtypeStruct((n, n_out), out_dtype)]
        out_specs = [pl.BlockSpec((tm, n_out), lambda i: (i, 0))]
    operands = [x, norm_w.reshape(1, d), w]
    in_specs = [pl.BlockSpec((tm, d), lambda i: (i, 0)), _resident((1, d)), _resident((d, n_out))]
    if w2 is not None:
        n2 = w2.shape[1]
        operands.append(w2)
        in_specs.append(_resident((d, n2)))
        out_shape.append(jax.ShapeDtypeStruct((n, n2), F32))
        out_specs.append(pl.BlockSpec((tm, n2), lambda i: (i, 0)))
    out = pl.pallas_call(
        functools.partial(_norm_matmul_kernel, tn=tn, tile_major=tile_major),
        out_shape=out_shape,
        grid=(n // tm,),
        in_specs=in_specs,
        out_specs=out_specs,
        compiler_params=_cparams(("parallel",)),
        name="norm_matmul",
    )(*operands)
    return out if w2 is not None else out[0]


def _ffn_kernel(x_ref, a_ref, wo_ref, nw_ref, wg_ref, wu_ref, wd_ref, fw_ref, o_ref, *, tf, final_norm):
    x = x_ref[...] + _dot(a_ref[...], wo_ref[...])
    y = x * lax.rsqrt(jnp.mean(x * x, axis=-1, keepdims=True) + EPS)
    h = (y * nw_ref[...]).astype(BF16)
    f = wg_ref.shape[1]
    acc = x
    for c0 in range(0, f, tf):
        g = _dot(h, wg_ref[:, c0:c0 + tf])
        u = _dot(h, wu_ref[:, c0:c0 + tf])
        a = (_silu(g) * u).astype(BF16)
        acc = acc + _dot(a, wd_ref[c0:c0 + tf, :])
    if final_norm:
        acc = acc * lax.rsqrt(jnp.mean(acc * acc, axis=-1, keepdims=True) + EPS) * fw_ref[...]
    o_ref[...] = acc


def mixer_out_ffn(x, a, wo, norm_w, wg, wu, wd, final_w, *, tm, tf, final_norm):
    n, d = x.shape
    f = wg.shape[1]
    ka = a.shape[1]
    return pl.pallas_call(
        functools.partial(_ffn_kernel, tf=tf, final_norm=final_norm),
        out_shape=jax.ShapeDtypeStruct((n, d), F32),
        grid=(n // tm,),
        in_specs=[
            pl.BlockSpec((tm, d), lambda i: (i, 0)),
            pl.BlockSpec((tm, ka), lambda i: (i, 0)),
            _resident((ka, d)),
            _resident((1, d)),
            _resident((d, f)),
            _resident((d, f)),
            _resident((f, d)),
            _resident((1, d)),
        ],
        out_specs=pl.BlockSpec((tm, d), lambda i: (i, 0)),
        compiler_params=_cparams(("parallel",)),
        name="mixer_out_ffn",
    )(x, a, wo, norm_w.reshape(1, d), wg, wu, wd, final_w.reshape(1, d))


GDN_TB = 128
GDN_LEVELS = GDN_TB.bit_length() - 1
GDN_HB = 8
GDN_QK_HEADS = 8
HIST = 8
BETA_LANE = 0
A_LANE = 16


def _gdn_kernel(alog_ref, dtb_ref, q_ref, k_ref, v_ref, z_ref, ba_ref, cwq_ref, cwk_ref, cwv_ref,
                nw_ref, o_ref, xbuf, s_ref):
    hq = pl.program_id(1)
    t = pl.program_id(2)
    tb = GDN_TB
    dh = GDN_HEAD_DIM
    hb = GDN_HB
    k_off = hb * dh
    v_off = 2 * hb * dh

    @pl.when(t == 0)
    def _():
        xbuf[0:HIST, :] = jnp.zeros((HIST, 4 * hb * dh), F32)
        s_ref[...] = jnp.zeros_like(s_ref)

    xbuf[HIST:HIST + tb, 0:k_off] = q_ref[...].astype(F32)
    xbuf[HIST:HIST + tb, k_off:v_off] = k_ref[...].astype(F32)
    xbuf[HIST:HIST + tb, v_off:] = v_ref[...].astype(F32)

    def conv(lo, width, cw_ref, cw_lo):
        acc = None
        for j in range(CONV_WIDTH):
            start = HIST - (CONV_WIDTH - 1) + j
            term = xbuf[start:start + tb, lo:lo + width] * cw_ref[j:j + 1, cw_lo:cw_lo + width]
            acc = term if acc is None else acc + term
        return _silu(acc)

    lane = lax.broadcasted_iota(jnp.int32, (1, LANES), 1)
    first = 2 * hb * hq
    ba = pltpu.roll(ba_ref[...], (LANES - first) % LANES, 1)
    beta = 1.0 / (1.0 + jnp.exp(-ba))
    alog_row = jnp.zeros((1, LANES), F32)
    dtb_row = jnp.zeros((1, LANES), F32)
    for i in range(2 * hb):
        alog_row = jnp.where(lane == A_LANE + i, alog_ref[first + i], alog_row)
        dtb_row = jnp.where(lane == A_LANE + i, dtb_ref[first + i], dtb_row)
    g = -jnp.exp(alog_row) * _softplus(ba + dtb_row)

    ii = lax.broadcasted_iota(jnp.int32, (tb, tb), 0)
    jj = lax.broadcasted_iota(jnp.int32, (tb, tb), 1)
    lower = ii >= jj
    strict = ii > jj
    diag = ii == jj
    ij_xor = jnp.bitwise_xor(ii, jj)
    tri = jnp.where(lower, 1.0, 0.0).astype(BF16)
    gc = sum(_dot(tri, p) for p in _split_bf16(g, 3))
    gc_t = gc.T
    nw = nw_ref[...]

    heads = range(2 * hb)
    q16, k16, kf, kk, qk = [], [], [], [], []
    for hl in range(hb):
        qc = conv(hl * dh, dh, cwq_ref, hl * dh)
        kc = conv(k_off + hl * dh, dh, cwk_ref, hl * dh)
        q = qc * lax.rsqrt(jnp.sum(qc * qc, axis=-1, keepdims=True) + EPS) * (dh ** -0.5)
        k = kc * lax.rsqrt(jnp.sum(kc * kc, axis=-1, keepdims=True) + EPS)
        kf.append(k)
        q16.append(q.astype(BF16))
        k16.append(k.astype(BF16))
        kk.append(jnp.where(strict, _dot_nt(k16[hl], k16[hl]), 0.0))
        qk.append(_dot_nt(q16[hl], k16[hl]))

    gcol = [gc[:, A_LANE + hv:A_LANE + hv + 1] for hv in heads]
    grow = [gc_t[A_LANE + hv:A_LANE + hv + 1, :] for hv in heads]
    bcol = [beta[:, BETA_LANE + hv:BETA_LANE + hv + 1] for hv in heads]
    glast = [grow[hv][:, tb - 1:tb] for hv in heads]
    decay = [jnp.where(lower, jnp.exp(gcol[hv] - grow[hv]), 0.0) for hv in heads]

    lmat = [(bcol[hv] * kk[hv // 2]) * decay[hv] for hv in heads]
    l16 = [lmat[hv].astype(BF16) for hv in heads]
    top_bit = (jnp.int32(31) - lax.clz(ij_xor))
    eye = jnp.where(diag, 1.0, 0.0)
    dinv = [eye - jnp.where(top_bit == 0, lmat[hv], 0.0) for hv in heads]
    for level in range(1, GDN_LEVELS):
        in_c = top_bit == level
        d16 = [dinv[hv].astype(BF16) for hv in heads]
        ld = [_dot(l16[hv], d16[hv]).astype(BF16) for hv in heads]
        dinv = [dinv[hv] - jnp.where(in_c, _dot(d16[hv], ld[hv]), 0.0) for hv in heads]
    r16 = [(dinv[hv] - eye).astype(BF16) for hv in heads]

    vb = [bcol[hv] * conv(v_off + hv * dh, dh, cwv_ref, hv * dh) for hv in heads]
    kb = [(bcol[hv] * jnp.exp(gcol[hv])) * kf[hv // 2] for hv in heads]
    ruw = [_dot(r16[hv], jnp.concatenate([vb[hv].astype(BF16), kb[hv].astype(BF16)], axis=1))
           for hv in heads]
    u = [vb[hv] + ruw[hv][:, :dh] for hv in heads]
    w = [kb[hv] + ruw[hv][:, dh:] for hv in heads]
    attn = [(qk[hv // 2] * decay[hv]).astype(BF16) for hv in heads]

    s = [s_ref[hv] for hv in heads]
    s16 = [s[hv].astype(BF16) for hv in heads]
    v_new = [u[hv] - _dot(w[hv].astype(BF16), s16[hv]) for hv in heads]
    o = [jnp.exp(gcol[hv]) * _dot(q16[hv // 2], s16[hv]) + _dot(attn[hv], v_new[hv].astype(BF16))
         for hv in heads]
    for hv in heads:
        k_dec = (jnp.exp(glast[hv] - gcol[hv]) * v_new[hv]).astype(BF16)
        s_ref[hv] = s[hv] * jnp.exp(glast[hv]) + _dot_tn(k16[hv // 2], k_dec)
    for hv in heads:
        zs = z_ref[:, hv * dh:(hv + 1) * dh].astype(F32)
        on = o[hv] * lax.rsqrt(jnp.mean(o[hv] * o[hv], axis=-1, keepdims=True) + EPS) * nw
        o_ref[:, hv * dh:(hv + 1) * dh] = (on * _silu(zs)).astype(o_ref.dtype)

    xbuf[0:HIST, :] = xbuf[tb:tb + HIST, :]


def gdn_core(proj, ba, conv_w_t, a_log, dt_bias, norm_w, *, batch, seq):
    n = proj.shape[0]
    dh = GDN_HEAD_DIM
    hb = GDN_HB
    ng = GDN_QK_HEADS // hb
    tpb = seq // GDN_TB
    row = lambda b, h, t: b * tpb + t
    return pl.pallas_call(
        _gdn_kernel,
        out_shape=jax.ShapeDtypeStruct((n, 2 * GDN_QK_HEADS * dh), BF16),
        grid=(batch, ng, tpb),
        in_specs=[
            pl.BlockSpec(memory_space=pltpu.SMEM),
            pl.BlockSpec(memory_space=pltpu.SMEM),
            pl.BlockSpec((GDN_TB, hb * dh), lambda b, h, t: (row(b, h, t), h)),
            pl.BlockSpec((GDN_TB, hb * dh), lambda b, h, t: (row(b, h, t), ng + h)),
            pl.BlockSpec((GDN_TB, 2 * hb * dh), lambda b, h, t: (row(b, h, t), ng + h)),
            pl.BlockSpec((GDN_TB, 2 * hb * dh), lambda b, h, t: (row(b, h, t), 2 * ng + h)),
            pl.BlockSpec((GDN_TB, LANES), lambda b, h, t: (row(b, h, t), 0)),
            pl.BlockSpec((CONV_WIDTH, hb * dh), lambda b, h, t: (0, h)),
            pl.BlockSpec((CONV_WIDTH, hb * dh), lambda b, h, t: (0, ng + h)),
            pl.BlockSpec((CONV_WIDTH, 2 * hb * dh), lambda b, h, t: (0, ng + h)),
            pl.BlockSpec((1, dh), lambda b, h, t: (0, 0)),
        ],
        out_specs=pl.BlockSpec((GDN_TB, 2 * hb * dh), lambda b, h, t: (row(b, h, t), h)),
        scratch_shapes=[
            pltpu.VMEM((GDN_TB + HIST, 4 * hb * dh), F32),
            pltpu.VMEM((2 * hb, dh, dh), F32),
        ],
        compiler_params=_cparams(("parallel", "parallel", "arbitrary")),
        name="gdn_core",
    )(a_log, dt_bias, proj, proj, proj, proj, ba, conv_w_t, conv_w_t, conv_w_t,
      norm_w.reshape(1, dh))


SB_T = 256
SB_NP = 2
SB_PAIRS = 8


def _sb_kernel(q_ref, k_ref, v_ref, o_ref):
    i = pl.program_id(2)
    tb, dh = SB_T, SB_HEAD_DIM
    lane = lax.broadcasted_iota(jnp.int32, (1, 2 * dh), 1)
    head_mask = [lane < dh, lane >= dh]
    zero16 = jnp.zeros((), BF16)
    scale = jnp.asarray(dh ** -0.5, BF16)
    heads = [(pr, h) for pr in range(SB_NP) for h in range(2)]
    qh = {(pr, h): jnp.where(head_mask[h], q_ref[pr] * scale, zero16) for pr, h in heads}

    ri = lax.broadcasted_iota(jnp.int32, (tb, tb), 0)
    ci = lax.broadcasted_iota(jnp.int32, (tb, tb), 1)
    upper = jnp.where(ri > ci, 1.0, 0.0).astype(BF16)
    upper2 = jnp.concatenate([upper, upper], axis=0)
    causal = ci < ri

    def visit(blocks, accs, sums):
        chains = [(b, hd) for b in range(len(blocks)) for hd in heads]
        kbs, vbs = {}, {}
        for b, (jb, _, _) in enumerate(blocks):
            s0 = pl.multiple_of(jb * tb, tb)
            for pr in range(SB_NP):
                kbs[b, pr] = k_ref[pr, pl.ds(s0, tb), :]
                vbs[b, pr] = v_ref[pr, pl.ds(s0, tb), :]
        z2 = {c: _dot_nt(qh[c[1]], kbs[c[0], c[1][0]]) for c in chains}
        sp2 = {c: _softplus(z2[c]) for c in chains}
        for c in chains:
            _, diagonal, exists = blocks[c[0]]
            if diagonal:
                sp2[c] = jnp.where(causal, sp2[c], 0.0)
            if exists is not None:
                sp2[c] = jnp.where(exists, sp2[c], 0.0)
        within = {}
        for c in chains:
            hi, lo = _split_bf16(sp2[c], 2)
            within[c] = _dot(jnp.concatenate([hi, lo], axis=1), upper2)
        total = {c: jnp.sum(sp2[c], axis=-1, keepdims=True) for c in chains}
        sums = dict(sums)
        att = {}
        for c in chains:
            att[c] = jnp.exp(z2[c] - sp2[c] - within[c] - sums[c[1]])
            sums[c[1]] = sums[c[1]] + total[c]
            _, diagonal, exists = blocks[c[0]]
            if diagonal:
                att[c] = jnp.where(causal, att[c], 0.0)
            if exists is not None:
                att[c] = jnp.where(exists, att[c], 0.0)
        accs = list(accs)
        for c in chains:
            pr, h = c[1]
            accs[pr] = accs[pr] + _dot(att[c].astype(BF16), jnp.where(head_mask[h], vbs[c[0], pr], zero16))
        return accs, sums

    zeros = jnp.zeros((tb, 1), F32)
    accs, sums = visit([(i, True, None), (jnp.maximum(i - 1, 0), False, i >= 1)],
                       [jnp.zeros((tb, 2 * dh), F32)] * SB_NP, {hd: zeros for hd in heads})

    def cond(carry):
        jb, _, sum_list = carry
        smallest = functools.reduce(jnp.minimum, [jnp.min(x) for x in sum_list])
        return jnp.logical_and(jb >= 0, smallest <= -EXP_ZERO_BELOW)

    def body(carry):
        jb, accs, sum_list = carry
        accs, sums = visit([(jb, False, None)], accs, dict(zip(heads, sum_list)))
        return jb - 1, accs, [sums[hd] for hd in heads]

    _, accs, _ = lax.while_loop(cond, body, (i - 2, accs, [sums[hd] for hd in heads]))
    for pr in range(SB_NP):
        o_ref[:, pr * LANES:(pr + 1) * LANES] = accs[pr].astype(o_ref.dtype)


def sb_attention(qkv, *, batch, seq):
    n = qkv.shape[1]
    ng = SB_PAIRS // SB_NP
    qpb = seq // SB_T
    return pl.pallas_call(
        _sb_kernel,
        out_shape=jax.ShapeDtypeStruct((n, SB_PAIRS * LANES), BF16),
        grid=(batch, ng, qpb),
        in_specs=[
            pl.BlockSpec((SB_NP, SB_T, LANES), lambda b, p, i: (p, b * qpb + i, 0)),
            pl.BlockSpec((SB_NP, seq, LANES), lambda b, p, i: (ng + p, b, 0)),
            pl.BlockSpec((SB_NP, seq, LANES), lambda b, p, i: (2 * ng + p, b, 0)),
        ],
        out_specs=pl.BlockSpec((SB_T, SB_NP * LANES), lambda b, p, i: (b * qpb + i, p)),
        compiler_params=_cparams(("parallel", "parallel", "arbitrary")),
        name="sb_attention",
    )(qkv, qkv, qkv)


def kernel(x, gdn_w_in, gdn_conv_w, gdn_a_log, gdn_dt_bias, gdn_norm_w, gdn_w_out, sb_w_qkv, sb_w_o,
           mix_norm_w, ffn_norm_w, ffn_w_gate, ffn_w_up, ffn_w_down, final_norm_w):
    batch, seq, d = x.shape
    n = batch * seq
    depth = mix_norm_w.shape[0]
    n_proj = 6 * d
    xf = x.reshape(n, d)
    for i in range(depth):
        j = i // 2
        if i % 2 == 0:
            w_in = gdn_w_in[j]
            w_main = w_in[:, :n_proj].astype(BF16)
            w_ba = jnp.pad(w_in[:, n_proj:], ((0, 0), (0, LANES - (w_in.shape[1] - n_proj)))).astype(BF16)
            proj, ba = norm_matmul(xf, mix_norm_w[i], w_main, w_ba, tm=512, tn=2 * MXU_N, out_dtype=BF16,
                                   tile_major=False)
            mixed = gdn_core(proj, ba, gdn_conv_w[j].T, gdn_a_log[j], gdn_dt_bias[j], gdn_norm_w[j],
                             batch=batch, seq=seq)
            w_out = gdn_w_out[j]
        else:
            qkv = norm_matmul(xf, mix_norm_w[i], sb_w_qkv[j].astype(BF16), tm=512, tn=2 * MXU_N,
                              out_dtype=BF16, tile_major=True)
            mixed = sb_attention(qkv, batch=batch, seq=seq)
            w_out = sb_w_o[j]
        xf = mixer_out_ffn(xf, mixed, w_out.astype(BF16), ffn_norm_w[i], ffn_w_gate[i].astype(BF16),
                           ffn_w_up[i].astype(BF16), ffn_w_down[i].astype(BF16), final_norm_w,
                           tm=512, tf=256, final_norm=(i == depth - 1))
    return xf.reshape(batch, seq, d)
```

```python
import functools

import jax
import jax.numpy as jnp
from jax import lax
from jax.experimental import pallas as pl
from jax.experimental.pallas import tpu as pltpu

F32 = jnp.float32
BF16 = jnp.bfloat16

EPS = 1e-6
GDN_HEAD_DIM = 128
SB_HEAD_DIM = 64
CONV_WIDTH = 4
LANES = 128
VMEM_LIMIT_BYTES = 48 * 1024 * 1024
EXP_ZERO_BELOW = -104.0


def _cparams(sem, flags=None):
    return pltpu.CompilerParams(dimension_semantics=sem, vmem_limit_bytes=VMEM_LIMIT_BYTES, flags=flags)


def _silu(x):
    return x * (1.0 / (1.0 + jnp.exp(-x)))


def _softplus(x):
    return jnp.maximum(x, 0.0) + jnp.log(1.0 + jnp.exp(-jnp.abs(x)))


def _dot(a, b):
    return jnp.dot(a, b, preferred_element_type=F32)


def _dot_nt(a, b):
    return lax.dot_general(a, b, (((1,), (1,)), ((), ())), preferred_element_type=F32)


def _dot_tn(a, b):
    return lax.dot_general(a, b, (((0,), (0,)), ((), ())), preferred_element_type=F32)


def _split_bf16(x, terms):
    parts = []
    r = x
    for _ in range(terms):
        p = r.astype(BF16)
        parts.append(p)
        r = r - p.astype(F32)
    return parts


MXU_N = 256


def _resident(block_shape):
    zeros = (0,) * len(block_shape)
    return pl.BlockSpec(block_shape, lambda i: zeros, pipeline_mode=pl.Buffered(1))


def _norm_matmul_kernel(x_ref, nw_ref, w_ref, *rest, tn, tile_major):
    if len(rest) == 3:
        w2_ref, o_ref, o2_ref = rest
    else:
        (o_ref,) = rest
        w2_ref = o2_ref = None
    x = x_ref[...]
    y = x * lax.rsqrt(jnp.mean(x * x, axis=-1, keepdims=True) + EPS)
    h = (y * nw_ref[...]).astype(BF16)
    n_out = w_ref.shape[1]
    for c0 in range(0, n_out, tn):
        r = _dot(h, w_ref[:, c0:c0 + tn]).astype(o_ref.dtype)
        if tile_major:
            for l0 in range(0, tn, LANES):
                o_ref[(c0 + l0) // LANES] = r[:, l0:l0 + LANES]
        else:
            o_ref[:, c0:c0 + tn] = r
    if w2_ref is not None:
        o2_ref[...] = _dot(h, w2_ref[...])


def norm_matmul(x, norm_w, w, w2=None, *, tm, tn, out_dtype, tile_major):
    n, d = x.shape
    n_out = w.shape[1]
    if tile_major:
        out_shape = [jax.ShapeDtypeStruct((n_out // LANES, n, LANES), out_dtype)]
        out_specs = [pl.BlockSpec((n_out // LANES, tm, LANES), lambda i: (0, i, 0))]
    else:
        out_shape = [jax.ShapeDtypeStruct((n, n_out), out_dtype)]
        out_specs = [pl.BlockSpec((tm, n_out), lambda i: (i, 0))]
    operands = [x, norm_w.reshape(1, d), w]
    in_specs = [pl.BlockSpec((tm, d), lambda i: (i, 0)), _resident((1, d)), _resident((d, n_out))]
    if w2 is not None:
        n2 = w2.shape[1]
        operands.append(w2)
        in_specs.append(_resident((d, n2)))
        out_shape.append(jax.ShapeDtypeStruct((n, n2), F32))
        out_specs.append(pl.BlockSpec((tm, n2), lambda i: (i, 0)))
    out = pl.pallas_call(
        functools.partial(_norm_matmul_kernel, tn=tn, tile_major=tile_major),
        out_shape=out_shape,
        grid=(n // tm,),
        in_specs=in_specs,
        out_specs=out_specs,
        compiler_params=_cparams(("parallel",)),
        name="norm_matmul",
    )(*operands)
    return out if w2 is not None else out[0]


def _ffn_kernel(x_ref, a_ref, wo_ref, nw_ref, wg_ref, wu_ref, wd_ref, fw_ref, o_ref, *, tf, final_norm):
    x = x_ref[...] + _dot(a_ref[...], wo_ref[...])
    y = x * lax.rsqrt(jnp.mean(x * x, axis=-1, keepdims=True) + EPS)
    h = (y * nw_ref[...]).astype(BF16)
    f = wg_ref.shape[1]
    acc = x
    for c0 in range(0, f, tf):
        g = _dot(h, wg_ref[:, c0:c0 + tf])
        u = _dot(h, wu_ref[:, c0:c0 + tf])
        a = (_silu(g) * u).astype(BF16)
        acc = acc + _dot(a, wd_ref[c0:c0 + tf, :])
    if final_norm:
        acc = acc * lax.rsqrt(jnp.mean(acc * acc, axis=-1, keepdims=True) + EPS) * fw_ref[...]
    o_ref[...] = acc


def mixer_out_ffn(x, a, wo, norm_w, wg, wu, wd, final_w, *, tm, tf, final_norm):
    n, d = x.shape
    f = wg.shape[1]
    ka = a.shape[1]
    return pl.pallas_call(
        functools.partial(_ffn_kernel, tf=tf, final_norm=final_norm),
        out_shape=jax.ShapeDtypeStruct((n, d), F32),
        grid=(n // tm,),
        in_specs=[
            pl.BlockSpec((tm, d), lambda i: (i, 0)),
            pl.BlockSpec((tm, ka), lambda i: (i, 0)),
            _resident((ka, d)),
            _resident((1, d)),
            _resident((d, f)),
            _resident((d, f)),
            _resident((f, d)),
            _resident((1, d)),
        ],
        out_specs=pl.BlockSpec((tm, d), lambda i: (i, 0)),
        compiler_params=_cparams(("parallel",)),
        name="mixer_out_ffn",
    )(x, a, wo, norm_w.reshape(1, d), wg, wu, wd, final_w.reshape(1, d))


GDN_TB = 128
GDN_LEVELS = GDN_TB.bit_length() - 1
GDN_HB = 8
GDN_QK_HEADS = 8
HIST = 8
BETA_LANE = 0
A_LANE = 16


def _gdn_kernel(alog_ref, dtb_ref, q_ref, k_ref, v_ref, z_ref, ba_ref, cwq_ref, cwk_ref, cwv_ref,
                nw_ref, o_ref, xbuf, s_ref):
    hq = pl.program_id(1)
    t = pl.program_id(2)
    tb = GDN_TB
    dh = GDN_HEAD_DIM
    hb = GDN_HB
    k_off = hb * dh
    v_off = 2 * hb * dh

    @pl.when(t == 0)
    def _():
        xbuf[0:HIST, :] = jnp.zeros((HIST, 4 * hb * dh), F32)
        s_ref[...] = jnp.zeros_like(s_ref)

    xbuf[HIST:HIST + tb, 0:k_off] = q_ref[...].astype(F32)
    xbuf[HIST:HIST + tb, k_off:v_off] = k_ref[...].astype(F32)
    xbuf[HIST:HIST + tb, v_off:] = v_ref[...].astype(F32)

    def conv(lo, width, cw_ref, cw_lo):
        acc = None
        for j in range(CONV_WIDTH):
            start = HIST - (CONV_WIDTH - 1) + j
            term = xbuf[start:start + tb, lo:lo + width] * cw_ref[j:j + 1, cw_lo:cw_lo + width]
            acc = term if acc is None else acc + term
        return _silu(acc)

    lane = lax.broadcasted_iota(jnp.int32, (1, LANES), 1)
    first = 2 * hb * hq
    ba = pltpu.roll(ba_ref[...], (LANES - first) % LANES, 1)
    beta = 1.0 / (1.0 + jnp.exp(-ba))
    alog_row = jnp.zeros((1, LANES), F32)
    dtb_row = jnp.zeros((1, LANES), F32)
    for i in range(2 * hb):
        alog_row = jnp.where(lane == A_LANE + i, alog_ref[first + i], alog_row)
        dtb_row = jnp.where(lane == A_LANE + i, dtb_ref[first + i], dtb_row)
    g = -jnp.exp(alog_row) * _softplus(ba + dtb_row)

    ii = lax.broadcasted_iota(jnp.int32, (tb, tb), 0)
    jj = lax.broadcasted_iota(jnp.int32, (tb, tb), 1)
    lower = ii >= jj
    strict = ii > jj
    diag = ii == jj
    ij_xor = jnp.bitwise_xor(ii, jj)
    tri = jnp.where(lower, 1.0, 0.0).astype(BF16)
    gc = sum(_dot(tri, p) for p in _split_bf16(g, 3))
    gc_t = gc.T
    nw = nw_ref[...]

    heads = range(2 * hb)
    q16, k16, kf, kk, qk = [], [], [], [], []
    for hl in range(hb):
        qc = conv(hl * dh, dh, cwq_ref, hl * dh)
        kc = conv(k_off + hl * dh, dh, cwk_ref, hl * dh)
        q = qc * lax.rsqrt(jnp.sum(qc * qc, axis=-1, keepdims=True) + EPS) * (dh ** -0.5)
        k = kc * lax.rsqrt(jnp.sum(kc * kc, axis=-1, keepdims=True) + EPS)
        kf.append(k)
        q16.append(q.astype(BF16))
        k16.append(k.astype(BF16))
        kk.append(jnp.where(strict, _dot_nt(k16[hl], k16[hl]), 0.0))
        qk.append(_dot_nt(q16[hl], k16[hl]))

    gcol = [gc[:, A_LANE + hv:A_LANE + hv + 1] for hv in heads]
    grow = [gc_t[A_LANE + hv:A_LANE + hv + 1, :] for hv in heads]
    bcol = [beta[:, BETA_LANE + hv:BETA_LANE + hv + 1] for hv in heads]
    glast = [grow[hv][:, tb - 1:tb] for hv in heads]
    decay = [jnp.where(lower, jnp.exp(gcol[hv] - grow[hv]), 0.0) for hv in heads]

    lmat = [(bcol[hv] * kk[hv // 2]) * decay[hv] for hv in heads]
    l16 = [lmat[hv].astype(BF16) for hv in heads]
    top_bit = (jnp.int32(31) - lax.clz(ij_xor))
    eye = jnp.where(diag, 1.0, 0.0)
    dinv = [eye - jnp.where(top_bit == 0, lmat[hv], 0.0) for hv in heads]
    for level in range(1, GDN_LEVELS):
        in_c = top_bit == level
        d16 = [dinv[hv].astype(BF16) for hv in heads]
        ld = [_dot(l16[hv], d16[hv]).astype(BF16) for hv in heads]
        dinv = [dinv[hv] - jnp.where(in_c, _dot(d16[hv], ld[hv]), 0.0) for hv in heads]
    r16 = [(dinv[hv] - eye).astype(BF16) for hv in heads]

    vb = [bcol[hv] * conv(v_off + hv * dh, dh, cwv_ref, hv * dh) for hv in heads]
    kb = [(bcol[hv] * jnp.exp(gcol[hv])) * kf[hv // 2] for hv in heads]
    ruw = [_dot(r16[hv], jnp.concatenate([vb[hv].astype(BF16), kb[hv].astype(BF16)], axis=1))
           for hv in heads]
    u = [vb[hv] + ruw[hv][:, :dh] for hv in heads]
    w = [kb[hv] + ruw[hv][:, dh:] for hv in heads]
    attn = [(qk[hv // 2] * decay[hv]).astype(BF16) for hv in heads]

    s = [s_ref[hv] for hv in heads]
    s16 = [s[hv].astype(BF16) for hv in heads]
    v_new = [u[hv] - _dot(w[hv].astype(BF16), s16[hv]) for hv in heads]
    o = [jnp.exp(gcol[hv]) * _dot(q16[hv // 2], s16[hv]) + _dot(attn[hv], v_new[hv].astype(BF16))
         for hv in heads]
    for hv in heads:
        k_dec = (jnp.exp(glast[hv] - gcol[hv]) * v_new[hv]).astype(BF16)
        s_ref[hv] = s[hv] * jnp.exp(glast[hv]) + _dot_tn(k16[hv // 2], k_dec)
    for hv in heads:
        zs = z_ref[:, hv * dh:(hv + 1) * dh].astype(F32)
        on = o[hv] * lax.rsqrt(jnp.mean(o[hv] * o[hv], axis=-1, keepdims=True) + EPS) * nw
        o_ref[:, hv * dh:(hv + 1) * dh] = (on * _silu(zs)).astype(o_ref.dtype)

    xbuf[0:HIST, :] = xbuf[tb:tb + HIST, :]


def gdn_core(proj, ba, conv_w_t, a_log, dt_bias, norm_w, *, batch, seq):
    n = proj.shape[0]
    dh = GDN_HEAD_DIM
    hb = GDN_HB
    ng = GDN_QK_HEADS // hb
    tpb = seq // GDN_TB
    row = lambda b, h, t: b * tpb + t
    return pl.pallas_call(
        _gdn_kernel,
        out_shape=jax.ShapeDtypeStruct((n, 2 * GDN_QK_HEADS * dh), BF16),
        grid=(batch, ng, tpb),
        in_specs=[
            pl.BlockSpec(memory_space=pltpu.SMEM),
            pl.BlockSpec(memory_space=pltpu.SMEM),
            pl.BlockSpec((GDN_TB, hb * dh), lambda b, h, t: (row(b, h, t), h)),
            pl.BlockSpec((GDN_TB, hb * dh), lambda b, h, t: (row(b, h, t), ng + h)),
            pl.BlockSpec((GDN_TB, 2 * hb * dh), lambda b, h, t: (row(b, h, t), ng + h)),
            pl.BlockSpec((GDN_TB, 2 * hb * dh), lambda b, h, t: (row(b, h, t), 2 * ng + h)),
            pl.BlockSpec((GDN_TB, LANES), lambda b, h, t: (row(b, h, t), 0)),
            pl.BlockSpec((CONV_WIDTH, hb * dh), lambda b, h, t: (0, h)),
            pl.BlockSpec((CONV_WIDTH, hb * dh), lambda b, h, t: (0, ng + h)),
            pl.BlockSpec((CONV_WIDTH, 2 * hb * dh), lambda b, h, t: (0, ng + h)),
            pl.BlockSpec((1, dh), lambda b, h, t: (0, 0)),
        ],
        out_specs=pl.BlockSpec((GDN_TB, 2 * hb * dh), lambda b, h, t: (row(b, h, t), h)),
        scratch_shapes=[
            pltpu.VMEM((GDN_TB + HIST, 4 * hb * dh), F32),
            pltpu.VMEM((2 * hb, dh, dh), F32),
        ],
        compiler_params=_cparams(("parallel", "parallel", "arbitrary")),
        name="gdn_core",
    )(a_log, dt_bias, proj, proj, proj, proj, ba, conv_w_t, conv_w_t, conv_w_t,
      norm_w.reshape(1, dh))


SB_T = 256
SB_FUSED_LEFT = 1
SB_NP = 2
SB_PAIRS = 8


def _sb_kernel(q_ref, k_ref, v_ref, o_ref):
    i = pl.program_id(2)
    tb, dh = SB_T, SB_HEAD_DIM
    lane = lax.broadcasted_iota(jnp.int32, (1, 2 * dh), 1)
    head_mask = [lane < dh, lane >= dh]
    zero16 = jnp.zeros((), BF16)
    scale = jnp.asarray(dh ** -0.5, BF16)
    heads = [(pr, h) for pr in range(SB_NP) for h in range(2)]
    qh = {(pr, h): jnp.where(head_mask[h], q_ref[pr] * scale, zero16) for pr, h in heads}

    ri = lax.broadcasted_iota(jnp.int32, (tb, tb), 0)
    ci = lax.broadcasted_iota(jnp.int32, (tb, tb), 1)
    upper = jnp.where(ri > ci, 1.0, 0.0).astype(BF16)
    causal = ci < ri

    def visit(blocks, accs, sums):
        chains = [(b, hd) for b in range(len(blocks)) for hd in heads]
        kbs, vbs = {}, {}
        for b, (jb, _, _) in enumerate(blocks):
            s0 = pl.multiple_of(jb * tb, tb)
            for pr in range(SB_NP):
                kbs[b, pr] = k_ref[pr, pl.ds(s0, tb), :]
                vbs[b, pr] = v_ref[pr, pl.ds(s0, tb), :]
        z2 = {c: _dot_nt(qh[c[1]], kbs[c[0], c[1][0]]) for c in chains}
        sp2 = {c: _softplus(z2[c]) for c in chains}
        for c in chains:
            if blocks[c[0]][1]:
                sp2[c] = jnp.where(causal, sp2[c], 0.0)
        within = {}
        for c in chains:
            within[c] = _dot(sp2[c].astype(BF16), upper)
        total = {c: jnp.sum(sp2[c], axis=-1, keepdims=True) for c in chains}
        sums = dict(sums)
        att = {}
        for c in chains:
            att[c] = jnp.exp(z2[c] - sp2[c] - within[c] - sums[c[1]])
            _, diagonal, exists = blocks[c[0]]
            if diagonal:
                att[c] = jnp.where(causal, att[c], 0.0)
            if exists is not None:
                total[c] = jnp.where(exists, total[c], 0.0)
            sums[c[1]] = sums[c[1]] + total[c]
        accs = list(accs)
        for c in chains:
            pr, h = c[1]
            exists = blocks[c[0]][2]
            keep = head_mask[h] if exists is None else jnp.logical_and(head_mask[h], exists)
            accs[pr] = accs[pr] + _dot(att[c].astype(BF16), jnp.where(keep, vbs[c[0], pr], zero16))
        return accs, sums

    zeros = jnp.zeros((tb, 1), F32)
    first = [(i, True, None)]
    first += [(jnp.maximum(i - d, 0), False, i >= d) for d in range(1, SB_FUSED_LEFT + 1)]
    accs, sums = visit(first, [jnp.zeros((tb, 2 * dh), F32)] * SB_NP, {hd: zeros for hd in heads})

    def cond(carry):
        jb, _, sum_list = carry
        smallest = functools.reduce(jnp.minimum, [jnp.min(x) for x in sum_list])
        return jnp.logical_and(jb >= 0, smallest <= -EXP_ZERO_BELOW)

    def body(carry):
        jb, accs, sum_list = carry
        accs, sums = visit([(jb, False, None)], accs, dict(zip(heads, sum_list)))
        return jb - 1, accs, [sums[hd] for hd in heads]

    _, accs, _ = lax.while_loop(cond, body, (i - 1 - SB_FUSED_LEFT, accs, [sums[hd] for hd in heads]))
    for pr in range(SB_NP):
        o_ref[:, pr * LANES:(pr + 1) * LANES] = accs[pr].astype(o_ref.dtype)


def sb_attention(qkv, *, batch, seq):
    n = qkv.shape[1]
    ng = SB_PAIRS // SB_NP
    qpb = seq // SB_T
    return pl.pallas_call(
        _sb_kernel,
        out_shape=jax.ShapeDtypeStruct((n, SB_PAIRS * LANES), BF16),
        grid=(batch, ng, qpb),
        in_specs=[
            pl.BlockSpec((SB_NP, SB_T, LANES), lambda b, p, i: (p, b * qpb + i, 0)),
            pl.BlockSpec((SB_NP, seq, LANES), lambda b, p, i: (ng + p, b, 0)),
            pl.BlockSpec((SB_NP, seq, LANES), lambda b, p, i: (2 * ng + p, b, 0)),
        ],
        out_specs=pl.BlockSpec((SB_T, SB_NP * LANES), lambda b, p, i: (b * qpb + i, p)),
        compiler_params=_cparams(("parallel", "parallel", "arbitrary")),
        name="sb_attention",
    )(qkv, qkv, qkv)


def kernel(x, gdn_w_in, gdn_conv_w, gdn_a_log, gdn_dt_bias, gdn_norm_w, gdn_w_out, sb_w_qkv, sb_w_o,
           mix_norm_w, ffn_norm_w, ffn_w_gate, ffn_w_up, ffn_w_down, final_norm_w):
    batch, seq, d = x.shape
    n = batch * seq
    depth = mix_norm_w.shape[0]
    n_proj = 6 * d
    xf = x.reshape(n, d)
    for i in range(depth):
        j = i // 2
        if i % 2 == 0:
            w_in = gdn_w_in[j]
            w_main = w_in[:, :n_proj].astype(BF16)
            w_ba = jnp.pad(w_in[:, n_proj:], ((0, 0), (0, LANES - (w_in.shape[1] - n_proj)))).astype(BF16)
            proj, ba = norm_matmul(xf, mix_norm_w[i], w_main, w_ba, tm=512, tn=2 * MXU_N, out_dtype=BF16,
                                   tile_major=False)
            mixed = gdn_core(proj, ba, gdn_conv_w[j].T, gdn_a_log[j], gdn_dt_bias[j], gdn_norm_w[j],
                             batch=batch, seq=seq)
            w_out = gdn_w_out[j]
        else:
            qkv = norm_matmul(xf, mix_norm_w[i], sb_w_qkv[j].astype(BF16), tm=512, tn=2 * MXU_N,
                              out_dtype=BF16, tile_major=True)
            mixed = sb_attention(qkv, batch=batch, seq=seq)
            w_out = sb_w_o[j]
        xf = mixer_out_ffn(xf, mixed, w_out.astype(BF16), ffn_norm_w[i], ffn_w_gate[i].astype(BF16),
                           ffn_w_up[i].astype(BF16), ffn_w_down[i].astype(BF16), final_norm_w,
                           tm=512, tf=256, final_norm=(i == depth - 1))
    return xf.reshape(batch, seq, d)
```

```python
import functools

import jax
import jax.numpy as jnp
from jax import lax
from jax.experimental import pallas as pl
from jax.experimental.pallas import tpu as pltpu

F32 = jnp.float32
BF16 = jnp.bfloat16

EPS = 1e-6
GDN_HEAD_DIM = 128
SB_HEAD_DIM = 64
CONV_WIDTH = 4
LANES = 128
VMEM_LIMIT_BYTES = 48 * 1024 * 1024
EXP_ZERO_BELOW = -104.0


def _cparams(sem, flags=None):
    return pltpu.CompilerParams(dimension_semantics=sem, vmem_limit_bytes=VMEM_LIMIT_BYTES, flags=flags)


def _silu(x):
    return x * (1.0 / (1.0 + jnp.exp(-x)))


def _softplus(x):
    return jnp.maximum(x, 0.0) + jnp.log(1.0 + jnp.exp(-jnp.abs(x)))


def _dot(a, b):
    return jnp.dot(a, b, preferred_element_type=F32)


def _dot_nt(a, b):
    return lax.dot_general(a, b, (((1,), (1,)), ((), ())), preferred_element_type=F32)


def _dot_tn(a, b):
    return lax.dot_general(a, b, (((0,), (0,)), ((), ())), preferred_element_type=F32)


def _split_bf16(x, terms):
    parts = []
    r = x
    for _ in range(terms):
        p = r.astype(BF16)
        parts.append(p)
        r = r - p.astype(F32)
    return parts


MXU_N = 256


def _resident(block_shape):
    zeros = (0,) * len(block_shape)
    return pl.BlockSpec(block_shape, lambda i: zeros, pipeline_mode=pl.Buffered(1))


def _norm_matmul_kernel(x_ref, nw_ref, w_ref, *rest, tn, tile_major):
    if len(rest) == 3:
        w2_ref, o_ref, o2_ref = rest
    else:
        (o_ref,) = rest
        w2_ref = o2_ref = None
    x = x_ref[...]
    y = x * lax.rsqrt(jnp.mean(x * x, axis=-1, keepdims=True) + EPS)
    h = (y * nw_ref[...]).astype(BF16)
    n_out = w_ref.shape[1]
    for c0 in range(0, n_out, tn):
        r = _dot(h, w_ref[:, c0:c0 + tn]).astype(o_ref.dtype)
        if tile_major:
            for l0 in range(0, tn, LANES):
                o_ref[(c0 + l0) // LANES] = r[:, l0:l0 + LANES]
        else:
            o_ref[:, c0:c0 + tn] = r
    if w2_ref is not None:
        o2_ref[...] = _dot(h, w2_ref[...])


def norm_matmul(x, norm_w, w, w2=None, *, tm, tn, out_dtype, tile_major):
    n, d = x.shape
    n_out = w.shape[1]
    if tile_major:
        out_shape = [jax.ShapeDtypeStruct((n_out // LANES, n, LANES), out_dtype)]
        out_specs = [pl.BlockSpec((n_out // LANES, tm, LANES), lambda i: (0, i, 0))]
    else:
        out_shape = [jax.ShapeDtypeStruct((n, n_out), out_dtype)]
        out_specs = [pl.BlockSpec((tm, n_out), lambda i: (i, 0))]
    operands = [x, norm_w.reshape(1, d), w]
    in_specs = [pl.BlockSpec((tm, d), lambda i: (i, 0)), _resident((1, d)), _resident((d, n_out))]
    if w2 is not None:
        n2 = w2.shape[1]
        operands.append(w2)
        in_specs.append(_resident((d, n2)))
        out_shape.append(jax.ShapeDtypeStruct((n, n2), F32))
        out_specs.append(pl.BlockSpec((tm, n2), lambda i: (i, 0)))
    out = pl.pallas_call(
        functools.partial(_norm_matmul_kernel, tn=tn, tile_major=tile_major),
        out_shape=out_shape,
        grid=(n // tm,),
        in_specs=in_specs,
        out_specs=out_specs,
        compiler_params=_cparams(("parallel",)),
        name="norm_matmul",
    )(*operands)
    return out if w2 is not None else out[0]


def _ffn_kernel(x_ref, a_ref, wo_ref, nw_ref, wg_ref, wu_ref, wd_ref, fw_ref, o_ref, *, tf, final_norm):
    x = x_ref[...] + _dot(a_ref[...], wo_ref[...])
    y = x * lax.rsqrt(jnp.mean(x * x, axis=-1, keepdims=True) + EPS)
    h = (y * nw_ref[...]).astype(BF16)
    f = wg_ref.shape[1]
    acc = x
    for c0 in range(0, f, tf):
        g = _dot(h, wg_ref[:, c0:c0 + tf])
        u = _dot(h, wu_ref[:, c0:c0 + tf])
        a = (_silu(g) * u).astype(BF16)
        acc = acc + _dot(a, wd_ref[c0:c0 + tf, :])
    if final_norm:
        acc = acc * lax.rsqrt(jnp.mean(acc * acc, axis=-1, keepdims=True) + EPS) * fw_ref[...]
    o_ref[...] = acc


def mixer_out_ffn(x, a, wo, norm_w, wg, wu, wd, final_w, *, tm, tf, final_norm):
    n, d = x.shape
    f = wg.shape[1]
    ka = a.shape[1]
    return pl.pallas_call(
        functools.partial(_ffn_kernel, tf=tf, final_norm=final_norm),
        out_shape=jax.ShapeDtypeStruct((n, d), F32),
        grid=(n // tm,),
        in_specs=[
            pl.BlockSpec((tm, d), lambda i: (i, 0)),
            pl.BlockSpec((tm, ka), lambda i: (i, 0)),
            _resident((ka, d)),
            _resident((1, d)),
            _resident((d, f)),
            _resident((d, f)),
            _resident((f, d)),
            _resident((1, d)),
        ],
        out_specs=pl.BlockSpec((tm, d), lambda i: (i, 0)),
        compiler_params=_cparams(("parallel",)),
        name="mixer_out_ffn",
    )(x, a, wo, norm_w.reshape(1, d), wg, wu, wd, final_w.reshape(1, d))


GDN_TB = 128
GDN_LEVELS = GDN_TB.bit_length() - 1
GDN_HB = 8
GDN_QK_HEADS = 8
HIST = 8
BETA_LANE = 0
A_LANE = 16


def _gdn_kernel(alog_ref, dtb_ref, q_ref, k_ref, v_ref, z_ref, ba_ref, cwq_ref, cwk_ref, cwv_ref,
                nw_ref, o_ref, xbuf, s_ref):
    hq = pl.program_id(1)
    t = pl.program_id(2)
    tb = GDN_TB
    dh = GDN_HEAD_DIM
    hb = GDN_HB
    k_off = hb * dh
    v_off = 2 * hb * dh

    @pl.when(t == 0)
    def _():
        xbuf[0:HIST, :] = jnp.zeros((HIST, 4 * hb * dh), F32)
        s_ref[...] = jnp.zeros_like(s_ref)

    xbuf[HIST:HIST + tb, 0:k_off] = q_ref[...].astype(F32)
    xbuf[HIST:HIST + tb, k_off:v_off] = k_ref[...].astype(F32)
    xbuf[HIST:HIST + tb, v_off:] = v_ref[...].astype(F32)

    def conv(lo, width, cw_ref, cw_lo):
        window = xbuf[:, lo:lo + width]
        acc = None
        for j in range(CONV_WIDTH):
            back = CONV_WIDTH - 1 - j
            shifted = window if back == 0 else pltpu.roll(window, back, 0)
            term = shifted[HIST:HIST + tb] * cw_ref[j:j + 1, cw_lo:cw_lo + width]
            acc = term if acc is None else acc + term
        return _silu(acc)

    lane = lax.broadcasted_iota(jnp.int32, (1, LANES), 1)
    first = 2 * hb * hq
    ba = pltpu.roll(ba_ref[...], (LANES - first) % LANES, 1)
    beta = 1.0 / (1.0 + jnp.exp(-ba))
    alog_row = jnp.zeros((1, LANES), F32)
    dtb_row = jnp.zeros((1, LANES), F32)
    for i in range(2 * hb):
        alog_row = jnp.where(lane == A_LANE + i, alog_ref[first + i], alog_row)
        dtb_row = jnp.where(lane == A_LANE + i, dtb_ref[first + i], dtb_row)
    g = -jnp.exp(alog_row) * _softplus(ba + dtb_row)

    ii = lax.broadcasted_iota(jnp.int32, (tb, tb), 0)
    jj = lax.broadcasted_iota(jnp.int32, (tb, tb), 1)
    lower = ii >= jj
    strict = ii > jj
    diag = ii == jj
    ij_xor = jnp.bitwise_xor(ii, jj)
    tri = jnp.where(lower, 1.0, 0.0).astype(BF16)
    gc = sum(_dot(tri, p) for p in _split_bf16(g, 3))
    gc_t = gc.T
    nw = nw_ref[...]

    heads = range(2 * hb)
    q16, k16, kf, kk, qk = [], [], [], [], []
    for hl in range(hb):
        qc = conv(hl * dh, dh, cwq_ref, hl * dh)
        kc = conv(k_off + hl * dh, dh, cwk_ref, hl * dh)
        q = qc * lax.rsqrt(jnp.sum(qc * qc, axis=-1, keepdims=True) + EPS) * (dh ** -0.5)
        k = kc * lax.rsqrt(jnp.sum(kc * kc, axis=-1, keepdims=True) + EPS)
        kf.append(k)
        q16.append(q.astype(BF16))
        k16.append(k.astype(BF16))
        kk.append(jnp.where(strict, _dot_nt(k16[hl], k16[hl]), 0.0))
        qk.append(_dot_nt(q16[hl], k16[hl]))

    gcol = [gc[:, A_LANE + hv:A_LANE + hv + 1] for hv in heads]
    grow = [gc_t[A_LANE + hv:A_LANE + hv + 1, :] for hv in heads]
    bcol = [beta[:, BETA_LANE + hv:BETA_LANE + hv + 1] for hv in heads]
    glast = [grow[hv][:, tb - 1:tb] for hv in heads]
    decay = [jnp.where(lower, jnp.exp(gcol[hv] - grow[hv]), 0.0) for hv in heads]

    lmat = [(bcol[hv] * kk[hv // 2]) * decay[hv] for hv in heads]
    l16 = [lmat[hv].astype(BF16) for hv in heads]
    top_bit = (jnp.int32(31) - lax.clz(ij_xor))
    eye = jnp.where(diag, 1.0, 0.0)
    dinv = [eye - jnp.where(top_bit == 0, lmat[hv], 0.0) for hv in heads]
    for level in range(1, GDN_LEVELS):
        in_c = top_bit == level
        d16 = [dinv[hv].astype(BF16) for hv in heads]
        ld = [_dot(l16[hv], d16[hv]).astype(BF16) for hv in heads]
        dinv = [dinv[hv] - jnp.where(in_c, _dot(d16[hv], ld[hv]), 0.0) for hv in heads]
    r16 = [(dinv[hv] - eye).astype(BF16) for hv in heads]

    vb = [bcol[hv] * conv(v_off + hv * dh, dh, cwv_ref, hv * dh) for hv in heads]
    kb = [(bcol[hv] * jnp.exp(gcol[hv])) * kf[hv // 2] for hv in heads]
    ruw = [_dot(r16[hv], jnp.concatenate([vb[hv].astype(BF16), kb[hv].astype(BF16)], axis=1))
           for hv in heads]
    u = [vb[hv] + ruw[hv][:, :dh] for hv in heads]
    w = [kb[hv] + ruw[hv][:, dh:] for hv in heads]
    attn = [(qk[hv // 2] * decay[hv]).astype(BF16) for hv in heads]

    s = [s_ref[hv] for hv in heads]
    s16 = [s[hv].astype(BF16) for hv in heads]
    v_new = [u[hv] - _dot(w[hv].astype(BF16), s16[hv]) for hv in heads]
    o = [jnp.exp(gcol[hv]) * _dot(q16[hv // 2], s16[hv]) + _dot(attn[hv], v_new[hv].astype(BF16))
         for hv in heads]
    for hv in heads:
        k_dec = (jnp.exp(glast[hv] - gcol[hv]) * v_new[hv]).astype(BF16)
        s_ref[hv] = s[hv] * jnp.exp(glast[hv]) + _dot_tn(k16[hv // 2], k_dec)
    for hv in heads:
        zs = z_ref[:, hv * dh:(hv + 1) * dh].astype(F32)
        on = o[hv] * lax.rsqrt(jnp.mean(o[hv] * o[hv], axis=-1, keepdims=True) + EPS) * nw
        o_ref[:, hv * dh:(hv + 1) * dh] = (on * _silu(zs)).astype(o_ref.dtype)

    xbuf[0:HIST, :] = xbuf[tb:tb + HIST, :]


def gdn_core(proj, ba, conv_w_t, a_log, dt_bias, norm_w, *, batch, seq):
    n = proj.shape[0]
    dh = GDN_HEAD_DIM
    hb = GDN_HB
    ng = GDN_QK_HEADS // hb
    tpb = seq // GDN_TB
    row = lambda b, h, t: b * tpb + t
    return pl.pallas_call(
        _gdn_kernel,
        out_shape=jax.ShapeDtypeStruct((n, 2 * GDN_QK_HEADS * dh), BF16),
        grid=(batch, ng, tpb),
        in_specs=[
            pl.BlockSpec(memory_space=pltpu.SMEM),
            pl.BlockSpec(memory_space=pltpu.SMEM),
            pl.BlockSpec((GDN_TB, hb * dh), lambda b, h, t: (row(b, h, t), h)),
            pl.BlockSpec((GDN_TB, hb * dh), lambda b, h, t: (row(b, h, t), ng + h)),
            pl.BlockSpec((GDN_TB, 2 * hb * dh), lambda b, h, t: (row(b, h, t), ng + h)),
            pl.BlockSpec((GDN_TB, 2 * hb * dh), lambda b, h, t: (row(b, h, t), 2 * ng + h)),
            pl.BlockSpec((GDN_TB, LANES), lambda b, h, t: (row(b, h, t), 0)),
            pl.BlockSpec((CONV_WIDTH, hb * dh), lambda b, h, t: (0, h)),
            pl.BlockSpec((CONV_WIDTH, hb * dh), lambda b, h, t: (0, ng + h)),
            pl.BlockSpec((CONV_WIDTH, 2 * hb * dh), lambda b, h, t: (0, ng + h)),
            pl.BlockSpec((1, dh), lambda b, h, t: (0, 0)),
        ],
        out_specs=pl.BlockSpec((GDN_TB, 2 * hb * dh), lambda b, h, t: (row(b, h, t), h)),
        scratch_shapes=[
            pltpu.VMEM((GDN_TB + HIST, 4 * hb * dh), F32),
            pltpu.VMEM((2 * hb, dh, dh), F32),
        ],
        compiler_params=_cparams(("parallel", "parallel", "arbitrary")),
        name="gdn_core",
    )(a_log, dt_bias, proj, proj, proj, proj, ba, conv_w_t, conv_w_t, conv_w_t,
      norm_w.reshape(1, dh))


SB_T = 256
SB_FUSED_LEFT = 1
SB_NP = 2
SB_PAIRS = 8


def _sb_kernel(q_ref, k_ref, v_ref, o_ref):
    i = pl.program_id(2)
    tb, dh = SB_T, SB_HEAD_DIM
    lane = lax.broadcasted_iota(jnp.int32, (1, 2 * dh), 1)
    head_mask = [lane < dh, lane >= dh]
    zero16 = jnp.zeros((), BF16)
    scale = jnp.asarray(dh ** -0.5, BF16)
    heads = [(pr, h) for pr in range(SB_NP) for h in range(2)]
    qh = {(pr, h): jnp.where(head_mask[h], q_ref[pr] * scale, zero16) for pr, h in heads}

    ri = lax.broadcasted_iota(jnp.int32, (tb, tb), 0)
    ci = lax.broadcasted_iota(jnp.int32, (tb, tb), 1)
    upper = jnp.where(ri > ci, 1.0, 0.0).astype(BF16)
    causal = ci < ri

    def visit(blocks, accs, sums):
        chains = [(b, hd) for b in range(len(blocks)) for hd in heads]
        kbs, vbs = {}, {}
        for b, (jb, _, _) in enumerate(blocks):
            s0 = pl.multiple_of(jb * tb, tb)
            for pr in range(SB_NP):
                kbs[b, pr] = k_ref[pr, pl.ds(s0, tb), :]
                vbs[b, pr] = v_ref[pr, pl.ds(s0, tb), :]
        z2 = {c: _dot_nt(qh[c[1]], kbs[c[0], c[1][0]]) for c in chains}
        sp2 = {c: _softplus(z2[c]) for c in chains}
        for c in chains:
            if blocks[c[0]][1]:
                sp2[c] = jnp.where(causal, sp2[c], 0.0)
        within = {}
        for c in chains:
            within[c] = _dot(sp2[c].astype(BF16), upper)
        total = {c: jnp.sum(sp2[c], axis=-1, keepdims=True) for c in chains}
        sums = dict(sums)
        att = {}
        for c in chains:
            att[c] = jnp.exp(z2[c] - sp2[c] - within[c] - sums[c[1]])
            _, diagonal, exists = blocks[c[0]]
            if diagonal:
                att[c] = jnp.where(causal, att[c], 0.0)
            if exists is not None:
                total[c] = jnp.where(exists, total[c], 0.0)
            sums[c[1]] = sums[c[1]] + total[c]
        accs = list(accs)
        for c in chains:
            pr, h = c[1]
            exists = blocks[c[0]][2]
            keep = head_mask[h] if exists is None else jnp.logical_and(head_mask[h], exists)
            accs[pr] = accs[pr] + _dot(att[c].astype(BF16), jnp.where(keep, vbs[c[0], pr], zero16))
        return accs, sums

    zeros = jnp.zeros((tb, 1), F32)
    first = [(i, True, None)]
    first += [(jnp.maximum(i - d, 0), False, i >= d) for d in range(1, SB_FUSED_LEFT + 1)]
    accs, sums = visit(first, [jnp.zeros((tb, 2 * dh), F32)] * SB_NP, {hd: zeros for hd in heads})

    def cond(carry):
        jb, _, sum_list = carry
        smallest = functools.reduce(jnp.minimum, [jnp.min(x) for x in sum_list])
        return jnp.logical_and(jb >= 0, smallest <= -EXP_ZERO_BELOW)

    def body(carry):
        jb, accs, sum_list = carry
        accs, sums = visit([(jb, False, None)], accs, dict(zip(heads, sum_list)))
        return jb - 1, accs, [sums[hd] for hd in heads]

    _, accs, _ = lax.while_loop(cond, body, (i - 1 - SB_FUSED_LEFT, accs, [sums[hd] for hd in heads]))
    for pr in range(SB_NP):
        o_ref[:, pr * LANES:(pr + 1) * LANES] = accs[pr].astype(o_ref.dtype)


def sb_attention(qkv, *, batch, seq):
    n = qkv.shape[1]
    ng = SB_PAIRS // SB_NP
    qpb = seq // SB_T
    return pl.pallas_call(
        _sb_kernel,
        out_shape=jax.ShapeDtypeStruct((n, SB_PAIRS * LANES), BF16),
        grid=(batch, ng, qpb),
        in_specs=[
            pl.BlockSpec((SB_NP, SB_T, LANES), lambda b, p, i: (p, b * qpb + i, 0)),
            pl.BlockSpec((SB_NP, seq, LANES), lambda b, p, i: (ng + p, b, 0)),
            pl.BlockSpec((SB_NP, seq, LANES), lambda b, p, i: (2 * ng + p, b, 0)),
        ],
        out_specs=pl.BlockSpec((SB_T, SB_NP * LANES), lambda b, p, i: (b * qpb + i, p)),
        compiler_params=_cparams(("parallel", "parallel", "arbitrary")),
        name="sb_attention",
    )(qkv, qkv, qkv)


def kernel(x, gdn_w_in, gdn_conv_w, gdn_a_log, gdn_dt_bias, gdn_norm_w, gdn_w_out, sb_w_qkv, sb_w_o,
           mix_norm_w, ffn_norm_w, ffn_w_gate, ffn_w_up, ffn_w_down, final_norm_w):
    batch, seq, d = x.shape
    n = batch * seq
    depth = mix_norm_w.shape[0]
    n_proj = 6 * d
    xf = x.reshape(n, d)
    for i in range(depth):
        j = i // 2
        if i % 2 == 0:
            w_in = gdn_w_in[j]
            w_main = w_in[:, :n_proj].astype(BF16)
            w_ba = jnp.pad(w_in[:, n_proj:], ((0, 0), (0, LANES - (w_in.shape[1] - n_proj)))).astype(BF16)
            proj, ba = norm_matmul(xf, mix_norm_w[i], w_main, w_ba, tm=512, tn=2 * MXU_N, out_dtype=BF16,
                                   tile_major=False)
            mixed = gdn_core(proj, ba, gdn_conv_w[j].T, gdn_a_log[j], gdn_dt_bias[j], gdn_norm_w[j],
                             batch=batch, seq=seq)
            w_out = gdn_w_out[j]
        else:
            qkv = norm_matmul(xf, mix_norm_w[i], sb_w_qkv[j].astype(BF16), tm=512, tn=2 * MXU_N,
                              out_dtype=BF16, tile_major=True)
            mixed = sb_attention(qkv, batch=batch, seq=seq)
            w_out = sb_w_o[j]
        xf = mixer_out_ffn(xf, mixed, w_out.astype(BF16), ffn_norm_w[i], ffn_w_gate[i].astype(BF16),
                           ffn_w_up[i].astype(BF16), ffn_w_down[i].astype(BF16), final_norm_w,
                           tm=512, tf=256, final_norm=(i == depth - 1))
    return xf.reshape(batch, seq, d)
```

```python
import functools

import jax
import jax.numpy as jnp
from jax import lax
from jax.experimental import pallas as pl
from jax.experimental.pallas import tpu as pltpu

F32 = jnp.float32
BF16 = jnp.bfloat16

EPS = 1e-6
GDN_HEAD_DIM = 128
SB_HEAD_DIM = 64
CONV_WIDTH = 4
LANES = 128
VMEM_LIMIT_BYTES = 48 * 1024 * 1024
EXP_ZERO_BELOW = -104.0


def _cparams(sem, flags=None):
    return pltpu.CompilerParams(dimension_semantics=sem, vmem_limit_bytes=VMEM_LIMIT_BYTES, flags=flags)


def _silu(x):
    return x * jax.nn.sigmoid(x)


def _softplus(x):
    return jnp.maximum(x, 0.0) + jnp.log(1.0 + jnp.exp(-jnp.abs(x)))


def _dot(a, b):
    return jnp.dot(a, b, preferred_element_type=F32)


def _dot_nt(a, b):
    return lax.dot_general(a, b, (((1,), (1,)), ((), ())), preferred_element_type=F32)


def _dot_tn(a, b):
    return lax.dot_general(a, b, (((0,), (0,)), ((), ())), preferred_element_type=F32)


def _split_bf16(x, terms):
    parts = []
    r = x
    for _ in range(terms):
        p = r.astype(BF16)
        parts.append(p)
        r = r - p.astype(F32)
    return parts


MXU_N = 256


def _resident(block_shape):
    zeros = (0,) * len(block_shape)
    return pl.BlockSpec(block_shape, lambda i: zeros, pipeline_mode=pl.Buffered(1))


def _norm_matmul_kernel(x_ref, nw_ref, w_ref, *rest, tn, tile_major):
    if len(rest) == 3:
        w2_ref, o_ref, o2_ref = rest
    else:
        (o_ref,) = rest
        w2_ref = o2_ref = None
    x = x_ref[...]
    y = x * lax.rsqrt(jnp.mean(x * x, axis=-1, keepdims=True) + EPS)
    h = (y * nw_ref[...]).astype(BF16)
    n_out = w_ref.shape[1]
    for c0 in range(0, n_out, tn):
        r = _dot(h, w_ref[:, c0:c0 + tn]).astype(o_ref.dtype)
        if tile_major:
            for l0 in range(0, tn, LANES):
                o_ref[(c0 + l0) // LANES] = r[:, l0:l0 + LANES]
        else:
            o_ref[:, c0:c0 + tn] = r
    if w2_ref is not None:
        o2_ref[...] = _dot(h, w2_ref[...])


def norm_matmul(x, norm_w, w, w2=None, *, tm, tn, out_dtype, tile_major):
    n, d = x.shape
    n_out = w.shape[1]
    if tile_major:
        out_shape = [jax.ShapeDtypeStruct((n_out // LANES, n, LANES), out_dtype)]
        out_specs = [pl.BlockSpec((n_out // LANES, tm, LANES), lambda i: (0, i, 0))]
    else:
        out_shape = [jax.ShapeDtypeStruct((n, n_out), out_dtype)]
        out_specs = [pl.BlockSpec((tm, n_out), lambda i: (i, 0))]
    operands = [x, norm_w.reshape(1, d), w]
    in_specs = [pl.BlockSpec((tm, d), lambda i: (i, 0)), _resident((1, d)), _resident((d, n_out))]
    if w2 is not None:
        n2 = w2.shape[1]
        operands.append(w2)
        in_specs.append(_resident((d, n2)))
        out_shape.append(jax.ShapeDtypeStruct((n, n2), F32))
        out_specs.append(pl.BlockSpec((tm, n2), lambda i: (i, 0)))
    out = pl.pallas_call(
        functools.partial(_norm_matmul_kernel, tn=tn, tile_major=tile_major),
        out_shape=out_shape,
        grid=(n // tm,),
        in_specs=in_specs,
        out_specs=out_specs,
        compiler_params=_cparams(("parallel",)),
        name="norm_matmul",
    )(*operands)
    return out if w2 is not None else out[0]


def _ffn_kernel(x_ref, a_ref, wo_ref, nw_ref, wg_ref, wu_ref, wd_ref, fw_ref, o_ref, *, tf, final_norm):
    x = x_ref[...] + _dot(a_ref[...], wo_ref[...])
    y = x * lax.rsqrt(jnp.mean(x * x, axis=-1, keepdims=True) + EPS)
    h = (y * nw_ref[...]).astype(BF16)
    f = wg_ref.shape[1]
    acc = x
    for c0 in range(0, f, tf):
        g = _dot(h, wg_ref[:, c0:c0 + tf])
        u = _dot(h, wu_ref[:, c0:c0 + tf])
        a = (_silu(g) * u).astype(BF16)
        acc = acc + _dot(a, wd_ref[c0:c0 + tf, :])
    if final_norm:
        acc = acc * lax.rsqrt(jnp.mean(acc * acc, axis=-1, keepdims=True) + EPS) * fw_ref[...]
    o_ref[...] = acc


def mixer_out_ffn(x, a, wo, norm_w, wg, wu, wd, final_w, *, tm, tf, final_norm):
    n, d = x.shape
    f = wg.shape[1]
    ka = a.shape[1]
    return pl.pallas_call(
        functools.partial(_ffn_kernel, tf=tf, final_norm=final_norm),
        out_shape=jax.ShapeDtypeStruct((n, d), F32),
        grid=(n // tm,),
        in_specs=[
            pl.BlockSpec((tm, d), lambda i: (i, 0)),
            pl.BlockSpec((tm, ka), lambda i: (i, 0)),
            _resident((ka, d)),
            _resident((1, d)),
            _resident((d, f)),
            _resident((d, f)),
            _resident((f, d)),
            _resident((1, d)),
        ],
        out_specs=pl.BlockSpec((tm, d), lambda i: (i, 0)),
        compiler_params=_cparams(("parallel",)),
        name="mixer_out_ffn",
    )(x, a, wo, norm_w.reshape(1, d), wg, wu, wd, final_w.reshape(1, d))


GDN_TB = 128
GDN_LEVELS = GDN_TB.bit_length() - 1
GDN_HB = 8
GDN_QK_HEADS = 8
HIST = 8
BETA_LANE = 0
A_LANE = 16


def _gdn_kernel(alog_ref, dtb_ref, q_ref, k_ref, v_ref, z_ref, ba_ref, cwq_ref, cwk_ref, cwv_ref,
                nw_ref, o_ref, xbuf, s_ref):
    hq = pl.program_id(1)
    t = pl.program_id(2)
    tb = GDN_TB
    dh = GDN_HEAD_DIM
    hb = GDN_HB
    k_off = hb * dh
    v_off = 2 * hb * dh

    @pl.when(t == 0)
    def _():
        xbuf[...] = jnp.zeros_like(xbuf)
        s_ref[...] = jnp.zeros_like(s_ref)

    def conv(src_ref, lo, hist_lo, width, cw_ref):
        window = jnp.concatenate([xbuf[:, hist_lo:hist_lo + width],
                                  src_ref[:, lo:lo + width].astype(F32)], axis=0)
        acc = None
        for j in range(CONV_WIDTH):
            back = CONV_WIDTH - 1 - j
            shifted = window if back == 0 else pltpu.roll(window, back, 0)
            term = shifted[HIST:HIST + tb] * cw_ref[j:j + 1, lo:lo + width]
            acc = term if acc is None else acc + term
        return _silu(acc)

    lane = lax.broadcasted_iota(jnp.int32, (1, LANES), 1)
    first = 2 * hb * hq
    ba = pltpu.roll(ba_ref[...], (LANES - first) % LANES, 1)
    beta = jax.nn.sigmoid(ba)
    alog_row = jnp.zeros((1, LANES), F32)
    dtb_row = jnp.zeros((1, LANES), F32)
    for i in range(2 * hb):
        alog_row = jnp.where(lane == A_LANE + i, alog_ref[first + i], alog_row)
        dtb_row = jnp.where(lane == A_LANE + i, dtb_ref[first + i], dtb_row)
    g = -jnp.exp(alog_row) * _softplus(ba + dtb_row)

    ii = lax.broadcasted_iota(jnp.int32, (tb, tb), 0)
    jj = lax.broadcasted_iota(jnp.int32, (tb, tb), 1)
    lower = ii >= jj
    strict = ii > jj
    diag = ii == jj
    ij_xor = jnp.bitwise_xor(ii, jj)
    tri = jnp.where(lower, 1.0, 0.0).astype(BF16)
    gc = sum(_dot(tri, p) for p in _split_bf16(g, 3))
    gc_t = gc.T
    nw = nw_ref[...]

    heads = range(2 * hb)
    q16, k16, kf, kk, qk = [], [], [], [], []
    for hl in range(hb):
        qc = conv(q_ref, hl * dh, hl * dh, dh, cwq_ref)
        kc = conv(k_ref, hl * dh, k_off + hl * dh, dh, cwk_ref)
        q = qc * lax.rsqrt(jnp.sum(qc * qc, axis=-1, keepdims=True) + EPS) * (dh ** -0.5)
        k = kc * lax.rsqrt(jnp.sum(kc * kc, axis=-1, keepdims=True) + EPS)
        kf.append(k)
        q16.append(q.astype(BF16))
        k16.append(k.astype(BF16))
        kk.append(jnp.where(strict, _dot_nt(k16[hl], k16[hl]), 0.0))
        qk.append(_dot_nt(q16[hl], k16[hl]))

    gcol = [gc[:, A_LANE + hv:A_LANE + hv + 1] for hv in heads]
    grow = [gc_t[A_LANE + hv:A_LANE + hv + 1, :] for hv in heads]
    bcol = [beta[:, BETA_LANE + hv:BETA_LANE + hv + 1] for hv in heads]
    glast = [grow[hv][:, tb - 1:tb] for hv in heads]
    decay = [jnp.where(lower, jnp.exp(gcol[hv] - grow[hv]), 0.0) for hv in heads]

    lmat = [(bcol[hv] * kk[hv // 2]) * decay[hv] for hv in heads]
    l16 = [lmat[hv].astype(BF16) for hv in heads]
    top_bit = (jnp.int32(31) - lax.clz(ij_xor))
    eye = jnp.where(diag, 1.0, 0.0)
    dinv = [eye - jnp.where(top_bit == 0, lmat[hv], 0.0) for hv in heads]
    for level in range(1, GDN_LEVELS):
        in_c = top_bit == level
        d16 = [dinv[hv].astype(BF16) for hv in heads]
        ld = [_dot(l16[hv], d16[hv]).astype(BF16) for hv in heads]
        dinv = [dinv[hv] - jnp.where(in_c, _dot(d16[hv], ld[hv]), 0.0) for hv in heads]
    r16 = [(dinv[hv] - eye).astype(BF16) for hv in heads]

    vb = [bcol[hv] * conv(v_ref, hv * dh, v_off + hv * dh, dh, cwv_ref) for hv in heads]
    kb = [(bcol[hv] * jnp.exp(gcol[hv])) * kf[hv // 2] for hv in heads]
    ruw = [_dot(r16[hv], jnp.concatenate([vb[hv].astype(BF16), kb[hv].astype(BF16)], axis=1))
           for hv in heads]
    u = [vb[hv] + ruw[hv][:, :dh] for hv in heads]
    w = [kb[hv] + ruw[hv][:, dh:] for hv in heads]
    attn = [(qk[hv // 2] * decay[hv]).astype(BF16) for hv in heads]

    s = [s_ref[hv] for hv in heads]
    s16 = [s[hv].astype(BF16) for hv in heads]
    v_new = [u[hv] - _dot(w[hv].astype(BF16), s16[hv]) for hv in heads]
    o = [jnp.exp(gcol[hv]) * _dot(q16[hv // 2], s16[hv]) + _dot(attn[hv], v_new[hv].astype(BF16))
         for hv in heads]
    for hv in heads:
        k_dec = (jnp.exp(glast[hv] - gcol[hv]) * v_new[hv]).astype(BF16)
        s_ref[hv] = s[hv] * jnp.exp(glast[hv]) + _dot_tn(k16[hv // 2], k_dec)
    for hv in heads:
        zs = z_ref[:, hv * dh:(hv + 1) * dh].astype(F32)
        on = o[hv] * lax.rsqrt(jnp.mean(o[hv] * o[hv], axis=-1, keepdims=True) + EPS) * nw
        o_ref[:, hv * dh:(hv + 1) * dh] = (on * _silu(zs)).astype(o_ref.dtype)

    for src_ref, lo in ((q_ref, 0), (k_ref, k_off), (v_ref, v_off)):
        tail = src_ref[tb - 2 * HIST:tb, :].astype(F32)
        xbuf[:, lo:lo + tail.shape[1]] = tail[HIST:, :]


def gdn_core(proj, ba, conv_w_t, a_log, dt_bias, norm_w, *, batch, seq):
    n = proj.shape[0]
    dh = GDN_HEAD_DIM
    hb = GDN_HB
    ng = GDN_QK_HEADS // hb
    tpb = seq // GDN_TB
    row = lambda b, h, t: b * tpb + t
    return pl.pallas_call(
        _gdn_kernel,
        out_shape=jax.ShapeDtypeStruct((n, 2 * GDN_QK_HEADS * dh), BF16),
        grid=(batch, ng, tpb),
        in_specs=[
            pl.BlockSpec(memory_space=pltpu.SMEM),
            pl.BlockSpec(memory_space=pltpu.SMEM),
            pl.BlockSpec((GDN_TB, hb * dh), lambda b, h, t: (row(b, h, t), h)),
            pl.BlockSpec((GDN_TB, hb * dh), lambda b, h, t: (row(b, h, t), ng + h)),
            pl.BlockSpec((GDN_TB, 2 * hb * dh), lambda b, h, t: (row(b, h, t), ng + h)),
            pl.BlockSpec((GDN_TB, 2 * hb * dh), lambda b, h, t: (row(b, h, t), 2 * ng + h)),
            pl.BlockSpec((GDN_TB, LANES), lambda b, h, t: (row(b, h, t), 0)),
            pl.BlockSpec((CONV_WIDTH, hb * dh), lambda b, h, t: (0, h)),
            pl.BlockSpec((CONV_WIDTH, hb * dh), lambda b, h, t: (0, ng + h)),
            pl.BlockSpec((CONV_WIDTH, 2 * hb * dh), lambda b, h, t: (0, ng + h)),
            pl.BlockSpec((1, dh), lambda b, h, t: (0, 0)),
        ],
        out_specs=pl.BlockSpec((GDN_TB, 2 * hb * dh), lambda b, h, t: (row(b, h, t), h)),
        scratch_shapes=[
            pltpu.VMEM((HIST, 4 * hb * dh), F32),
            pltpu.VMEM((2 * hb, dh, dh), F32),
        ],
        compiler_params=_cparams(("parallel", "parallel", "arbitrary")),
        name="gdn_core",
    )(a_log, dt_bias, proj, proj, proj, proj, ba, conv_w_t, conv_w_t, conv_w_t,
      norm_w.reshape(1, dh))


SB_T = 256
SB_FUSED_LEFT = 1
SB_NP = 4
SB_PAIRS = 8


def _sb_kernel(q_ref, k_ref, v_ref, o_ref):
    i = pl.program_id(2)
    tb, dh = SB_T, SB_HEAD_DIM
    lane = lax.broadcasted_iota(jnp.int32, (1, 2 * dh), 1)
    head_mask = [lane < dh, lane >= dh]
    zero16 = jnp.zeros((), BF16)
    scale = jnp.asarray(dh ** -0.5, BF16)
    heads = [(pr, h) for pr in range(SB_NP) for h in range(2)]
    qh = {(pr, h): jnp.where(head_mask[h], q_ref[pr] * scale, zero16) for pr, h in heads}

    ri = lax.broadcasted_iota(jnp.int32, (tb, tb), 0)
    ci = lax.broadcasted_iota(jnp.int32, (tb, tb), 1)
    upper = jnp.where(ri > ci, 1.0, 0.0).astype(BF16)
    causal = ci < ri

    def visit(blocks, accs, sums):
        chains = [(b, hd) for b in range(len(blocks)) for hd in heads]
        kbs, vbs = {}, {}
        for b, (jb, _, _) in enumerate(blocks):
            s0 = pl.multiple_of(jb * tb, tb)
            for pr in range(SB_NP):
                kbs[b, pr] = k_ref[pr, pl.ds(s0, tb), :]
                vbs[b, pr] = v_ref[pr, pl.ds(s0, tb), :]
        z2 = {c: _dot_nt(qh[c[1]], kbs[c[0], c[1][0]]) for c in chains}
        sp2 = {c: _softplus(z2[c]) for c in chains}
        for c in chains:
            if blocks[c[0]][1]:
                sp2[c] = jnp.where(causal, sp2[c], 0.0)
        within = {}
        for c in chains:
            within[c] = _dot(sp2[c].astype(BF16), upper)
        total = {c: jnp.sum(sp2[c], axis=-1, keepdims=True) for c in chains}
        sums = dict(sums)
        att = {}
        for c in chains:
            att[c] = jnp.exp(z2[c] - sp2[c] - within[c] - sums[c[1]])
            _, diagonal, exists = blocks[c[0]]
            if diagonal:
                att[c] = jnp.where(causal, att[c], 0.0)
            if exists is not None:
                total[c] = jnp.where(exists, total[c], 0.0)
            sums[c[1]] = sums[c[1]] + total[c]
        accs = list(accs)
        for c in chains:
            pr, h = c[1]
            exists = blocks[c[0]][2]
            keep = head_mask[h] if exists is None else jnp.logical_and(head_mask[h], exists)
            accs[pr] = accs[pr] + _dot(att[c].astype(BF16), jnp.where(keep, vbs[c[0], pr], zero16))
        return accs, sums

    zeros = jnp.zeros((tb, 1), F32)
    first = [(i, True, None)]
    first += [(jnp.maximum(i - d, 0), False, i >= d) for d in range(1, SB_FUSED_LEFT + 1)]
    accs, sums = visit(first, [jnp.zeros((tb, 2 * dh), F32)] * SB_NP, {hd: zeros for hd in heads})

    def cond(carry):
        jb, _, sum_list = carry
        smallest = functools.reduce(jnp.minimum, [jnp.min(x) for x in sum_list])
        return jnp.logical_and(jb >= 0, smallest <= -EXP_ZERO_BELOW)

    def body(carry):
        jb, accs, sum_list = carry
        accs, sums = visit([(jb, False, None)], accs, dict(zip(heads, sum_list)))
        return jb - 1, accs, [sums[hd] for hd in heads]

    _, accs, _ = lax.while_loop(cond, body, (i - 1 - SB_FUSED_LEFT, accs, [sums[hd] for hd in heads]))
    for pr in range(SB_NP):
        o_ref[:, pr * LANES:(pr + 1) * LANES] = accs[pr].astype(o_ref.dtype)


def sb_attention(qkv, *, batch, seq):
    n = qkv.shape[1]
    ng = SB_PAIRS // SB_NP
    qpb = seq // SB_T
    return pl.pallas_call(
        _sb_kernel,
        out_shape=jax.ShapeDtypeStruct((n, SB_PAIRS * LANES), BF16),
        grid=(batch, ng, qpb),
        in_specs=[
            pl.BlockSpec((SB_NP, SB_T, LANES), lambda b, p, i: (p, b * qpb + i, 0)),
            pl.BlockSpec((SB_NP, seq, LANES), lambda b, p, i: (ng + p, b, 0)),
            pl.BlockSpec((SB_NP, seq, LANES), lambda b, p, i: (2 * ng + p, b, 0)),
        ],
        out_specs=pl.BlockSpec((SB_T, SB_NP * LANES), lambda b, p, i: (b * qpb + i, p)),
        compiler_params=_cparams(("parallel", "parallel", "arbitrary")),
        name="sb_attention",
    )(qkv, qkv, qkv)


def kernel(x, gdn_w_in, gdn_conv_w, gdn_a_log, gdn_dt_bias, gdn_norm_w, gdn_w_out, sb_w_qkv, sb_w_o,
           mix_norm_w, ffn_norm_w, ffn_w_gate, ffn_w_up, ffn_w_down, final_norm_w):
    batch, seq, d = x.shape
    n = batch * seq
    depth = mix_norm_w.shape[0]
    n_proj = 6 * d
    xf = x.reshape(n, d)
    for i in range(depth):
        j = i // 2
        if i % 2 == 0:
            w_in = gdn_w_in[j]
            w_main = w_in[:, :n_proj].astype(BF16)
            w_ba = jnp.pad(w_in[:, n_proj:], ((0, 0), (0, LANES - (w_in.shape[1] - n_proj)))).astype(BF16)
            proj, ba = norm_matmul(xf, mix_norm_w[i], w_main, w_ba, tm=512, tn=2 * MXU_N, out_dtype=BF16,
                                   tile_major=False)
            mixed = gdn_core(proj, ba, gdn_conv_w[j].T, gdn_a_log[j], gdn_dt_bias[j], gdn_norm_w[j],
                             batch=batch, seq=seq)
            w_out = gdn_w_out[j]
        else:
            qkv = norm_matmul(xf, mix_norm_w[i], sb_w_qkv[j].astype(BF16), tm=512, tn=2 * MXU_N,
                              out_dtype=BF16, tile_major=True)
            mixed = sb_attention(qkv, batch=batch, seq=seq)
            w_out = sb_w_o[j]
        xf = mixer_out_ffn(xf, mixed, w_out.astype(BF16), ffn_norm_w[i], ffn_w_gate[i].astype(BF16),
                           ffn_w_up[i].astype(BF16), ffn_w_down[i].astype(BF16), final_norm_w,
                           tm=512, tf=256, final_norm=(i == depth - 1))
    return xf.reshape(batch, seq, d)
```
